```python
import math
import numpy as np
import jax
import jax.numpy as jnp
from jax import lax

D_MODEL = 1024
BATCH = 8
SEQ = 8192
DEPTH = 2

GRID_W = 64
CTX_LEN = 256
EPS = 1e-6
N_MOD = 6

GLA_HEADS = 4
GLA_DK = 64
GLA_DV = 128
GLA_GATE_RANK = 16
GLA_GATE_TEMP = 16.0
GLA_CHUNK = 64
GLA_QK = GLA_HEADS * GLA_DK
GLA_V = GLA_HEADS * GLA_DV

ATT_Q_HEADS = 8
ATT_KV_HEADS = 2
ATT_HD = 64
ATT_WINDOW = 128
ATT_BLOCK = 128
ROPE_BASE = 10000.0
ATT_Q = ATT_Q_HEADS * ATT_HD
ATT_KV = ATT_KV_HEADS * ATT_HD

SSD_HEADS = 16
SSD_HD = 64
SSD_GROUPS = 2
SSD_STATE = 128
SSD_CONV = 5
SSD_CHUNK = 128
SSD_INNER = SSD_HEADS * SSD_HD
SSD_BC = SSD_GROUPS * SSD_STATE
SSD_XBC = SSD_INNER + 2 * SSD_BC
SSD_HPG = SSD_HEADS // SSD_GROUPS

N_BRANCH = 3

N_EXPERTS = 16
N_EXPERT_GROUPS = 4
EXPERTS_PER_GROUP = N_EXPERTS // N_EXPERT_GROUPS
TOP_K = 2
EXPERT_FF = 1024
MOE_BLOCK = 128

IN_NAMES = ('gla_q', 'gla_k', 'gla_v', 'gla_r', 'gla_gf', 'gla_gb', 'att_q', 'att_k', 'att_v', 'ssd_z', 'ssd_xbc', 'ssd_dtf', 'ssd_dtb', 'merge')
IN_SPLITS = (GLA_QK, GLA_QK, GLA_V, GLA_V, GLA_GATE_RANK, GLA_GATE_RANK, ATT_Q, ATT_KV, ATT_KV, SSD_INNER, SSD_XBC, SSD_HEADS, SSD_HEADS, N_BRANCH * D_MODEL)
N_IN = sum(IN_SPLITS)

kernel_name = 'hybrid_gla_swa_ssd_moe_prefix_dit'

F32 = jnp.float32


def rms_norm(x, g):
    xf = x.astype(F32)
    y = xf * lax.rsqrt(jnp.mean(xf * xf, axis=-1, keepdims=True) + EPS)
    return (y * g.astype(F32)).astype(x.dtype)


def modulate(x, g, shift, scale):
    return (rms_norm(x, g) * (1.0 + scale) + shift).astype(x.dtype)


def split_proj(p):
    offsets = np.cumsum(IN_SPLITS)[:-1].tolist()
    return dict(zip(IN_NAMES, jnp.split(p, offsets, axis=-1)))


def axial_rope_tables(n):
    rows = n // GRID_W
    row = jnp.repeat(jnp.arange(rows), GRID_W).astype(F32)
    col = jnp.tile(jnp.arange(GRID_W), rows).astype(F32)
    axis_dim = ATT_HD // 2
    inv_freq = ROPE_BASE ** (-jnp.arange(0, axis_dim, 2, dtype=F32) / axis_dim)
    ang_r = row[:, None] * inv_freq
    ang_c = col[:, None] * inv_freq
    return (jnp.cos(ang_r), jnp.sin(ang_r), jnp.cos(ang_c), jnp.sin(ang_c))


def _rotate_half(x, cos, sin):
    x1, x2 = jnp.split(x, 2, axis=-1)
    return jnp.concatenate([x1 * cos - x2 * sin, x1 * sin + x2 * cos], axis=-1)


def apply_axial_rope(x, cos_r, sin_r, cos_c, sin_c):
    xr, xc = jnp.split(x.astype(F32), 2, axis=-1)
    yr = _rotate_half(xr, cos_r[:, None, :], sin_r[:, None, :])
    yc = _rotate_half(xc, cos_c[:, None, :], sin_c[:, None, :])
    return jnp.concatenate([yr, yc], axis=-1).astype(x.dtype)


def gla_chunk_scan(q, k, v, g, s0):
    bsz, n, nh, _ = q.shape
    dv = v.shape[-1]
    nc = n // GLA_CHUNK

    def chunks(a):
        return a.astype(F32).reshape(bsz, nc, GLA_CHUNK, nh, a.shape[-1]).transpose(1, 0, 3, 2, 4)

    causal = jnp.tril(jnp.ones((GLA_CHUNK, GLA_CHUNK), dtype=bool))[:, :, None]

    def step(state, inp):
        qc, kc, vc, gc = inp
        b = jnp.cumsum(gc, axis=2)
        inter = jnp.einsum('bhtd,bhde->bhte', qc * jnp.exp(b), state)
        diff = b[:, :, :, None, :] - b[:, :, None, :, :]
        decay = jnp.exp(jnp.where(causal, diff, -jnp.inf))
        scores = jnp.einsum('bhtd,bhsd,bhtsd->bhts', qc, kc, decay)
        intra = jnp.einsum('bhts,bhse->bhte', scores, vc)
        b_last = b[:, :, -1:, :]
        new_state = jnp.exp(b_last[:, :, 0, :])[..., None] * state + jnp.einsum('bhsd,bhse->bhde', kc * jnp.exp(b_last - b), vc)
        return new_state, inter + intra

    s_fin, out = lax.scan(step, s0, (chunks(q), chunks(k), chunks(v), chunks(g)))
    out = out.transpose(1, 0, 3, 2, 4).reshape(bsz, n, nh, dv)
    return out, s_fin


def gla_bidirectional(q, k, v, g_fwd, g_bwd, s0_fwd, s0_bwd):
    flip = lambda a: a[:, ::-1]
    o_f, s_f = gla_chunk_scan(q, k, v, g_fwd, s0_fwd)
    o_b, s_b = gla_chunk_scan(flip(q), flip(k), flip(v), flip(g_bwd), s0_bwd)
    return o_f + flip(o_b), s_f, s_b


def _gla_prep(p, lp):
    bsz, n = p['gla_q'].shape[:2]
    hs = (bsz, n, GLA_HEADS)
    q = p['gla_q'].reshape(*hs, GLA_DK) * (GLA_DK ** -0.5)
    k = p['gla_k'].reshape(*hs, GLA_DK)
    v = p['gla_v'].reshape(*hs, GLA_DV)

    def gate(lowrank, d):
        logits = (lowrank @ lp['gla_w2'][d] + lp['gla_b2'][d]).astype(F32)
        return (jax.nn.log_sigmoid(logits) / GLA_GATE_TEMP).reshape(*hs, GLA_DK)

    return q, k, v, gate(p['gla_gf'], 0), gate(p['gla_gb'], 1)


def _gla_out(o, r, lp):
    bsz, n = o.shape[:2]
    o = rms_norm(o, lp['gla_norm_g']).reshape(bsz, n, GLA_V)
    return (o * jax.nn.silu(r.astype(F32))).astype(r.dtype)


def gla_branch(pc, pl, lp, need_ctx):
    qc, kc, vc, gfc, gbc = _gla_prep(pc, lp)
    s0 = jnp.zeros((qc.shape[0], GLA_HEADS, GLA_DK, GLA_DV), F32)
    oc, s_f, s_b = gla_bidirectional(qc, kc, vc, gfc, gbc, s0, s0)
    ql, kl, vl, gfl, gbl = _gla_prep(pl, lp)
    ol, _, _ = gla_bidirectional(ql, kl, vl, gfl, gbl, s_f, s_b)
    y_ctx = _gla_out(oc, pc['gla_r'], lp) if need_ctx else None
    return y_ctx, _gla_out(ol, pl['gla_r'], lp)


def sink_attention(q, k, v, sink, mask):
    bsz, lq = q.shape[:2]
    rep = ATT_Q_HEADS // ATT_KV_HEADS
    qg = q.reshape(bsz, lq, ATT_KV_HEADS, rep, ATT_HD)
    s = jnp.einsum('blgrd,bmgd->bgrlm', qg, k, preferred_element_type=F32) * (ATT_HD ** -0.5)
    if mask is not None:
        s = jnp.where(mask, s, -jnp.inf)
    sk = sink.astype(F32).reshape(1, ATT_KV_HEADS, rep, 1, 1)
    m = jnp.maximum(jnp.max(s, axis=-1, keepdims=True), sk)
    p = jnp.exp(s - m)
    w = p / (jnp.sum(p, axis=-1, keepdims=True) + jnp.exp(sk - m))
    o = jnp.einsum('bgrlm,bmgd->blgrd', w.astype(v.dtype), v)
    return o.reshape(bsz, lq, ATT_Q)


def windowed_latent_attention(q, k, v, k_ctx, v_ctx, sink):
    bsz, n = q.shape[:2]
    nb = n // ATT_BLOCK
    span = ATT_BLOCK + 2 * ATT_WINDOW
    pad = ((0, 0), (ATT_WINDOW, ATT_WINDOW), (0, 0), (0, 0))
    kp = jnp.pad(k, pad)
    vp = jnp.pad(v, pad)
    qb = jnp.moveaxis(q.reshape(bsz, nb, ATT_BLOCK, ATT_Q_HEADS, ATT_HD), 1, 0)
    a = jnp.arange(ATT_BLOCK)[:, None]
    j = jnp.arange(span)[None, :]
    band = (j >= a) & (j <= a + 2 * ATT_WINDOW)
    ctx_mask = jnp.ones((ATT_BLOCK, k_ctx.shape[1]), dtype=bool)

    def one_block(args):
        qi, i = args
        start = i * ATT_BLOCK
        kw = lax.dynamic_slice_in_dim(kp, start, span, axis=1)
        vw = lax.dynamic_slice_in_dim(vp, start, span, axis=1)
        pos = start - ATT_WINDOW + jnp.arange(span)
        valid = (pos >= 0) & (pos < n)
        mask = jnp.concatenate([ctx_mask, band & valid[None, :]], axis=1)
        keys = jnp.concatenate([k_ctx, kw], axis=1)
        vals = jnp.concatenate([v_ctx, vw], axis=1)
        return sink_attention(qi, keys, vals, sink, mask)

    out = lax.map(one_block, (qb, jnp.arange(nb)))
    return jnp.moveaxis(out, 0, 1).reshape(bsz, n, ATT_Q)


def attn_branch(pc, pl, lp, rope, need_ctx):
    def prep(p):
        bsz, n = p['att_q'].shape[:2]
        q = rms_norm(p['att_q'].reshape(bsz, n, ATT_Q_HEADS, ATT_HD), lp['q_norm_g'])
        k = rms_norm(p['att_k'].reshape(bsz, n, ATT_KV_HEADS, ATT_HD), lp['k_norm_g'])
        v = p['att_v'].reshape(bsz, n, ATT_KV_HEADS, ATT_HD)
        return q, k, v

    qc, kc, vc = prep(pc)
    ql, kl, vl = prep(pl)
    ql = apply_axial_rope(ql, *rope)
    kl = apply_axial_rope(kl, *rope)
    y_lat = windowed_latent_attention(ql, kl, vl, kc, vc, lp['att_sink'])
    y_ctx = sink_attention(qc, kc, vc, lp['att_sink'], None) if need_ctx else None
    return y_ctx, y_lat


def dwconv_centred(x, w, b):
    pad = SSD_CONV // 2
    y = lax.conv_general_dilated(x, w[:, None, :].astype(x.dtype), window_strides=(1,), padding=[(pad, pad)], dimension_numbers=('NWC', 'WIO', 'NWC'), feature_group_count=x.shape[-1])
    return y + b.astype(x.dtype)


def ssd_chunk_scan(xs, dt, a, bm, cm, h0):
    bsz, n = xs.shape[:2]
    nc = n // SSD_CHUNK

    def chunks(t):
        return jnp.moveaxis(t.astype(F32).reshape(bsz, nc, SSD_CHUNK, *t.shape[2:]), 1, 0)

    causal = jnp.tril(jnp.ones((SSD_CHUNK, SSD_CHUNK), dtype=bool))[None, :, :, None]

    def step(h, inp):
        xc, dtc, ac, bc, cc = inp
        cum = jnp.cumsum(ac, axis=1)
        seg = cum[:, :, None, :] - cum[:, None, :, :]
        decay = jnp.exp(jnp.where(causal, seg, -jnp.inf))
        cb = jnp.repeat(jnp.einsum('btgn,bsgn->btsg', cc, bc), SSD_HPG, axis=-1)
        y_intra = jnp.einsum('btsh,bsh,bshp->bthp', cb * decay, dtc, xc)
        bc_h = jnp.repeat(bc, SSD_HPG, axis=2)
        cc_h = jnp.repeat(cc, SSD_HPG, axis=2)
        y_inter = jnp.einsum('bthn,bhpn->bthp', cc_h, h) * jnp.exp(cum)[..., None]
        w = jnp.exp(cum[:, -1:, :] - cum) * dtc
        h_new = jnp.exp(cum[:, -1, :])[:, :, None, None] * h + jnp.einsum('bsh,bshn,bshp->bhpn', w, bc_h, xc)
        return h_new, y_intra + y_inter

    h_fin, y = lax.scan(step, h0, (chunks(xs), chunks(dt), chunks(a), chunks(bm), chunks(cm)))
    return jnp.moveaxis(y, 0, 1).reshape(xs.shape), h_fin


def _ssd_core(p, lp, h0_f, h0_b):
    bsz, n = p['ssd_xbc'].shape[:2]
    xbc = jax.nn.silu(dwconv_centred(p['ssd_xbc'], lp['ssd_conv_w'], lp['ssd_conv_b']))
    xs, bm, cm = jnp.split(xbc, [SSD_INNER, SSD_INNER + SSD_BC], axis=-1)
    xs = xs.reshape(bsz, n, SSD_HEADS, SSD_HD)
    bm = bm.reshape(bsz, n, SSD_GROUPS, SSD_STATE)
    cm = cm.reshape(bsz, n, SSD_GROUPS, SSD_STATE)

    def dir_params(raw, d):
        dt = jax.nn.softplus(raw.astype(F32) + lp['ssd_dt_bias'][d].astype(F32))
        return dt, dt * -jnp.exp(lp['ssd_a_log'][d].astype(F32))

    dt_f, a_f = dir_params(p['ssd_dtf'], 0)
    dt_b, a_b = dir_params(p['ssd_dtb'], 1)
    flip = lambda t: t[:, ::-1]
    y_f, h_f = ssd_chunk_scan(xs, dt_f, a_f, bm, cm, h0_f)
    y_b, h_b = ssd_chunk_scan(flip(xs), flip(dt_b), flip(a_b), flip(bm), flip(cm), h0_b)
    y = y_f + flip(y_b) + lp['ssd_d'].astype(F32)[:, None] * xs.astype(F32)
    return y, h_f, h_b


def _ssd_out(y, z, lp):
    bsz, n = z.shape[:2]
    gsz = SSD_INNER // SSD_GROUPS
    yz = y.reshape(bsz, n, SSD_GROUPS, gsz) * jax.nn.silu(z.astype(F32)).reshape(bsz, n, SSD_GROUPS, gsz)
    return rms_norm(yz, lp['ssd_norm_g'].reshape(SSD_GROUPS, gsz)).reshape(bsz, n, SSD_INNER).astype(z.dtype)


def ssd_branch(pc, pl, lp, need_ctx):
    bsz = pc['ssd_xbc'].shape[0]
    h0 = jnp.zeros((bsz, SSD_HEADS, SSD_HD, SSD_STATE), F32)
    yc, h_f, h_b = _ssd_core(pc, lp, h0, h0)
    yl, _, _ = _ssd_core(pl, lp, h_f, h_b)
    y_ctx = _ssd_out(yc, pc['ssd_z'], lp) if need_ctx else None
    return y_ctx, _ssd_out(yl, pl['ssd_z'], lp)


def merge_branches(p, y_gla, y_att, y_ssd, lp):
    bsz, n = y_gla.shape[:2]
    gates = jax.nn.sigmoid((p['merge'] + lp['gate_b'].reshape(-1)).astype(F32)).reshape(bsz, n, N_BRANCH, D_MODEL)
    merged = (gates[:, :, 0] * (y_gla @ lp['w_br_a'])
              + gates[:, :, 1] * (y_att @ lp['w_br_b'])
              + gates[:, :, 2] * (y_ssd @ lp['w_br_c']))
    return merged.astype(y_gla.dtype) @ lp['w_out']


def moe_ffn(h, router_w, router_b, w1, w3, w2):
    bsz, n, d = h.shape
    tok_x = h.reshape(-1, d)
    n_tok = tok_x.shape[0]
    scores = jax.nn.sigmoid(jnp.dot(tok_x, router_w, preferred_element_type=F32))
    biased = (scores + router_b.astype(F32)).reshape(n_tok, N_EXPERT_GROUPS, EXPERTS_PER_GROUP)
    group_score = jnp.sum(lax.top_k(biased, TOP_K)[0], axis=-1)
    group = jnp.argmax(group_score, axis=-1)
    in_group = jnp.take_along_axis(biased, group[:, None, None], axis=1)[:, 0]
    local = lax.top_k(in_group, TOP_K)[1]
    expert = group[:, None] * EXPERTS_PER_GROUP + local
    gate = jnp.take_along_axis(scores, expert, axis=1)
    gate = gate / jnp.sum(gate, axis=-1, keepdims=True)
    flat_e = expert.reshape(-1)
    n_assign = flat_e.shape[0]
    order = jnp.argsort(flat_e)
    sorted_e = flat_e[order]
    src_tok = order // TOP_K
    counts = jnp.bincount(flat_e, length=N_EXPERTS)
    padded = (counts + MOE_BLOCK - 1) // MOE_BLOCK * MOE_BLOCK
    pad_end = jnp.cumsum(padded)
    pad_start = pad_end - padded
    start = jnp.cumsum(counts) - counts
    dest = pad_start[sorted_e] + jnp.arange(n_assign) - start[sorted_e]
    n_blocks = -(-n_assign // MOE_BLOCK) + N_EXPERTS
    buf = jnp.zeros((n_blocks * MOE_BLOCK, d), h.dtype).at[dest].set(tok_x[src_tok])
    block_e = jnp.minimum(jnp.searchsorted(pad_end, jnp.arange(n_blocks) * MOE_BLOCK, side='right'), N_EXPERTS - 1)

    def expert_block(args):
        xb, e = args
        hid = jax.nn.silu(xb @ w1[e]) * (xb @ w3[e])
        return hid @ w2[e]

    out = lax.map(expert_block, (buf.reshape(n_blocks, MOE_BLOCK, d), block_e)).reshape(-1, d)
    y = out[dest].astype(F32) * gate.reshape(-1)[order][:, None]
    return jax.ops.segment_sum(y, src_tok, num_segments=n_tok).astype(h.dtype).reshape(bsz, n, d)


def trunk_layer(x, ctx, c, c_ctx, lp, router_w, router_b, rope, need_ctx):
    mod_lat = jnp.split((jax.nn.silu(c) @ lp['w_mod'] + lp['b_mod'])[:, None, :], N_MOD, axis=-1)
    mod_ctx = jnp.split((jax.nn.silu(c_ctx) @ lp['w_mod'] + lp['b_mod'])[None, None, :], N_MOD, axis=-1)
    shift1, scale1, gate1, shift2, scale2, gate2 = mod_lat
    cshift1, cscale1, cgate1, cshift2, cscale2, cgate2 = mod_ctx

    h_lat = modulate(x, lp['norm1_g'], shift1, scale1)
    h_ctx = modulate(ctx, lp['norm1_g'], cshift1, cscale1)
    p_lat = split_proj(h_lat @ lp['w_in'])
    p_ctx = split_proj(h_ctx @ lp['w_in'])

    gla_ctx, gla_lat = gla_branch(p_ctx, p_lat, lp, need_ctx)
    att_ctx, att_lat = attn_branch(p_ctx, p_lat, lp, rope, need_ctx)
    ssd_ctx, ssd_lat = ssd_branch(p_ctx, p_lat, lp, need_ctx)

    x = x + (gate1 * merge_branches(p_lat, gla_lat, att_lat, ssd_lat, lp)).astype(x.dtype)
    if not need_ctx:
        ffn = moe_ffn(modulate(x, lp['norm2_g'], shift2, scale2), router_w, router_b, lp['w1'], lp['w3'], lp['w2'])
        return x + (gate2 * ffn).astype(x.dtype), ctx

    ctx = ctx + (cgate1 * merge_branches(p_ctx, gla_ctx, att_ctx, ssd_ctx, lp)).astype(ctx.dtype)
    h2 = jnp.concatenate([modulate(ctx, lp['norm2_g'], cshift2, cscale2), modulate(x, lp['norm2_g'], shift2, scale2)], axis=1)
    ffn = moe_ffn(h2, router_w, router_b, lp['w1'], lp['w3'], lp['w2'])
    n_ctx = ctx.shape[1]
    ctx = ctx + (cgate2 * ffn[:, :n_ctx]).astype(ctx.dtype)
    x = x + (gate2 * ffn[:, n_ctx:]).astype(x.dtype)
    return x, ctx


def setup_inputs(seed: int = 0) -> dict:
    key = jax.random.key(seed)
    ks = jax.random.split(key, 31)
    D = D_MODEL
    L = DEPTH

    def nrm(k, shape, s):
        return jax.random.normal(k, shape, jnp.float32) * s

    dt = jnp.exp(jax.random.uniform(ks[16], (L, 2, SSD_HEADS), minval=math.log(1e-3), maxval=math.log(1e-1)))
    return {
        'x': nrm(ks[0], (BATCH, SEQ, D), 1.0),
        'c': nrm(ks[1], (BATCH, D), 1.0),
        'ctx': nrm(ks[2], (BATCH, CTX_LEN, D), 1.0),
        'c_ctx': nrm(ks[3], (D,), 1.0),
        'w_mod': nrm(ks[4], (L, D, N_MOD * D), 0.02),
        'b_mod': nrm(ks[5], (L, N_MOD * D), 0.02),
        'norm1_g': 1.0 + nrm(ks[6], (L, D), 0.05),
        'w_in': nrm(ks[7], (L, D, N_IN), D ** -0.5),
        'gla_w2': nrm(ks[8], (L, 2, GLA_GATE_RANK, GLA_QK), GLA_GATE_RANK ** -0.5),
        'gla_b2': nrm(ks[9], (L, 2, GLA_QK), 0.1),
        'gla_norm_g': 1.0 + nrm(ks[10], (L, GLA_DV), 0.05),
        'q_norm_g': 1.0 + nrm(ks[11], (L, ATT_HD), 0.05),
        'k_norm_g': 1.0 + nrm(ks[12], (L, ATT_HD), 0.05),
        'att_sink': nrm(ks[13], (L, ATT_Q_HEADS), 0.5),
        'ssd_conv_w': nrm(ks[14], (L, SSD_CONV, SSD_XBC), SSD_CONV ** -0.5),
        'ssd_conv_b': nrm(ks[15], (L, SSD_XBC), 0.02),
        'ssd_dt_bias': dt + jnp.log(-jnp.expm1(-dt)),
        'ssd_a_log': jnp.log(jax.random.uniform(ks[17], (L, 2, SSD_HEADS), minval=1.0, maxval=16.0)),
        'ssd_d': 1.0 + nrm(ks[18], (L, SSD_HEADS), 0.05),
        'ssd_norm_g': 1.0 + nrm(ks[19], (L, SSD_INNER), 0.05),
        'gate_b': nrm(ks[20], (L, N_BRANCH, D), 0.1),
        'w_br_a': nrm(ks[21], (L, GLA_V, D), GLA_V ** -0.5),
        'w_br_b': nrm(ks[22], (L, ATT_Q, D), ATT_Q ** -0.5),
        'w_br_c': nrm(ks[23], (L, SSD_INNER, D), SSD_INNER ** -0.5),
        'w_out': nrm(ks[24], (L, D, D), D ** -0.5),
        'norm2_g': 1.0 + nrm(ks[25], (L, D), 0.05),
        'router_w': nrm(ks[26], (D, N_EXPERTS), D ** -0.5),
        'router_b': nrm(ks[27], (N_EXPERTS,), 0.01),
        'w1': nrm(ks[28], (L, N_EXPERTS, D, EXPERT_FF), D ** -0.5),
        'w3': nrm(ks[29], (L, N_EXPERTS, D, EXPERT_FF), D ** -0.5),
        'w2': nrm(ks[30], (L, N_EXPERTS, EXPERT_FF, D), EXPERT_FF ** -0.5),
    }


def reference(x, c, ctx, c_ctx, w_mod, b_mod, norm1_g, w_in, gla_w2, gla_b2, gla_norm_g, q_norm_g, k_norm_g, att_sink, ssd_conv_w, ssd_conv_b, ssd_dt_bias, ssd_a_log, ssd_d, ssd_norm_g, gate_b, w_br_a, w_br_b, w_br_c, w_out, norm2_g, router_w, router_b, w1, w3, w2):
    rope = axial_rope_tables(x.shape[1])
    for l in range(DEPTH):
        lp = {
            'w_mod': w_mod[l], 'b_mod': b_mod[l], 'norm1_g': norm1_g[l], 'w_in': w_in[l],
            'gla_w2': gla_w2[l], 'gla_b2': gla_b2[l], 'gla_norm_g': gla_norm_g[l],
            'q_norm_g': q_norm_g[l], 'k_norm_g': k_norm_g[l], 'att_sink': att_sink[l],
            'ssd_conv_w': ssd_conv_w[l], 'ssd_conv_b': ssd_conv_b[l], 'ssd_dt_bias': ssd_dt_bias[l],
            'ssd_a_log': ssd_a_log[l], 'ssd_d': ssd_d[l], 'ssd_norm_g': ssd_norm_g[l],
            'gate_b': gate_b[l], 'w_br_a': w_br_a[l], 'w_br_b': w_br_b[l], 'w_br_c': w_br_c[l],
            'w_out': w_out[l], 'norm2_g': norm2_g[l], 'w1': w1[l], 'w3': w3[l], 'w2': w2[l],
        }
        x, ctx = trunk_layer(x, ctx, c, c_ctx, lp, router_w, router_b, rope, l < DEPTH - 1)
    return x
```

```python
import functools

import numpy as np
import jax
import jax.numpy as jnp
from jax import lax
from jax.experimental import pallas as pl
from jax.experimental.pallas import tpu as pltpu

F32 = jnp.float32
BF16 = jnp.bfloat16

D_MODEL = 1024
DEPTH = 2
GRID_W = 64
EPS = 1e-6
N_MOD = 6

GLA_HEADS = 4
GLA_DK = 64
GLA_DV = 128
GLA_GATE_RANK = 16
GLA_GATE_TEMP = 16.0
GLA_CHUNK = 64
GLA_SUB = 16
GLA_QK = GLA_HEADS * GLA_DK
GLA_V = GLA_HEADS * GLA_DV

ATT_Q_HEADS = 8
ATT_KV_HEADS = 2
ATT_HD = 64
ATT_WINDOW = 128
ATT_BLOCK = 128
ROPE_BASE = 10000.0
ATT_Q = ATT_Q_HEADS * ATT_HD
ATT_KV = ATT_KV_HEADS * ATT_HD

SSD_HEADS = 16
SSD_HD = 64
SSD_GROUPS = 2
SSD_STATE = 128
SSD_CONV = 5
SSD_CHUNK = 128
SSD_INNER = SSD_HEADS * SSD_HD
SSD_BC = SSD_GROUPS * SSD_STATE
SSD_XBC = SSD_INNER + 2 * SSD_BC
SSD_HPG = SSD_HEADS // SSD_GROUPS

N_BRANCH = 3
N_EXPERTS = 16
N_EXPERT_GROUPS = 4
EXPERTS_PER_GROUP = N_EXPERTS // N_EXPERT_GROUPS
EXPERT_FF = 1024

IN_NAMES = ('gla_q', 'gla_k', 'gla_v', 'gla_r', 'gla_gf', 'gla_gb', 'att_q', 'att_k', 'att_v',
            'ssd_z', 'ssd_xbc', 'ssd_dtf', 'ssd_dtb', 'merge')
IN_SPLITS = (GLA_QK, GLA_QK, GLA_V, GLA_V, GLA_GATE_RANK, GLA_GATE_RANK, ATT_Q, ATT_KV, ATT_KV,
             SSD_INNER, SSD_XBC, SSD_HEADS, SSD_HEADS, N_BRANCH * D_MODEL)

P_ORDER = ('merge', 'ssd_z', 'ssd_xbc', 'gla_v', 'gla_r', 'att_q', 'gla_q', 'gla_k', 'att_k', 'att_v',
           'gla_gf', 'gla_gb', 'ssd_dtf', 'ssd_dtb')
LANE = 128
SMALL_PAD = 4 * 16
N_P = sum(IN_SPLITS) + SMALL_PAD


def _p_offsets():
    width = dict(zip(IN_NAMES, IN_SPLITS))
    off, o = {}, 0
    for name in P_ORDER:
        off[name] = o
        o += width[name]
    return off


P_OFF = _p_offsets()
SMALL_OFF = P_OFF['gla_gf']
SM_GF, SM_GB, SM_DTF, SM_DTB = 0, 16, 32, 48

ROW_TILE = 256
MOE_BLOCK = 256
NEG = -1e30
VMEM_LIMIT = 56 * 1024 * 1024

NN = (((1,), (0,)), ((), ()))
NT = (((1,), (1,)), ((), ()))
TN = (((0,), (0,)), ((), ()))


def _dot(a, b, dims=NN):
    return lax.dot_general(a.astype(BF16), b.astype(BF16), dims, preferred_element_type=F32)


def _dot_hi(a, b, dims=NN):
    return lax.dot_general(a.astype(F32), b.astype(F32), dims, precision=lax.Precision.HIGHEST,
                           preferred_element_type=F32)


def _sigmoid(x):
    return 1.0 / (1.0 + jnp.exp(-x))


def _silu(x):
    return x * _sigmoid(x)


def _softplus(x):
    return jnp.maximum(x, 0.0) + jnp.log(1.0 + jnp.exp(-jnp.abs(x)))


def _iota(shape, dim):
    return lax.broadcasted_iota(jnp.int32, shape, dim)


def _cparams(n_axes):
    return pltpu.CompilerParams(dimension_semantics=("arbitrary",) * n_axes, vmem_limit_bytes=VMEM_LIMIT)


def _largest_tile(n, cap, mult):
    t = (min(cap, n) // mult) * mult
    while n % t:
        t -= mult
    return t


def _mod_kernel(c_ref, w_ref, b_ref, o_ref):
    o_ref[0] = _dot_hi(_silu(c_ref[...]), w_ref[0]) + b_ref[0]


def _mod_vectors(cc, w_mod, b_mod):
    n_l, d, n6 = w_mod.shape
    tn = 1024
    return pl.pallas_call(
        _mod_kernel,
        grid=(n_l, n6 // tn),
        in_specs=[pl.BlockSpec(cc.shape, lambda l, j: (0, 0)),
                  pl.BlockSpec((1, d, tn), lambda l, j: (l, 0, j)),
                  pl.BlockSpec((1, 1, tn), lambda l, j: (l, 0, j))],
        out_specs=pl.BlockSpec((1, cc.shape[0], tn), lambda l, j: (l, 0, j)),
        out_shape=jax.ShapeDtypeStruct((n_l, cc.shape[0], n6), F32),
        compiler_params=_cparams(2),
        name="mod_vectors",
    )(cc, w_mod, b_mod.reshape(n_l, 1, n6))


def _mod_spec(which, nct):
    return pl.BlockSpec((1, 1, D_MODEL), lambda b, j: (2 * b + jnp.where(j >= nct, 1, 0), 0, which))


def _normmod(x, g, shift, scale):
    y = x * lax.rsqrt(jnp.mean(x * x, axis=-1, keepdims=True) + EPS) * g
    return y * (1.0 + scale) + shift


def _normmod_kernel(x_ref, g_ref, sh_ref, sc_ref, o_ref):
    o_ref[...] = _normmod(x_ref[...], g_ref[...], sh_ref[0], sc_ref[0]).astype(o_ref.dtype)


def _norm_modulate(xc, g, mod, bsz, nt_rows, nct):
    tiles = nt_rows // ROW_TILE
    return pl.pallas_call(
        _normmod_kernel,
        grid=(bsz, tiles),
        in_specs=[pl.BlockSpec((ROW_TILE, D_MODEL), lambda b, j: (b * tiles + j, 0)),
                  pl.BlockSpec((1, D_MODEL), lambda b, j: (0, 0)),
                  _mod_spec(0, nct), _mod_spec(1, nct)],
        out_specs=pl.BlockSpec((ROW_TILE, D_MODEL), lambda b, j: (b * tiles + j, 0)),
        out_shape=jax.ShapeDtypeStruct(xc.shape, BF16),
        compiler_params=_cparams(2),
        name="norm_modulate",
    )(xc, g.reshape(1, -1), mod, mod)


def _matmul_kernel(a_ref, w_ref, o_ref):
    o_ref[...] = jnp.dot(a_ref[...], w_ref[...], preferred_element_type=F32).astype(o_ref.dtype)


def _in_proj(h, w):
    m, kdim = h.shape
    n = w.shape[1]
    tm = _largest_tile(m, 2048, 256)
    tn = 1152
    return pl.pallas_call(
        _matmul_kernel,
        grid=(m // tm, n // tn),
        in_specs=[pl.BlockSpec((tm, kdim), lambda i, j: (i, 0)),
                  pl.BlockSpec((kdim, tn), lambda i, j: (0, j))],
        out_specs=pl.BlockSpec((tm, tn), lambda i, j: (i, j)),
        out_shape=jax.ShapeDtypeStruct((m, n), F32),
        compiler_params=_cparams(2),
        name="in_proj",
    )(h, w)


def _chunk_order(j, nc_ctx, nc, rev):
    if not rev:
        return j
    return jnp.where(j < nc_ctx, nc_ctx - 1 - j, nc - 1 - (j - nc_ctx))


def _gla_kernel(q_ref, k_ref, v_ref, sm_ref, w2_ref, b2_ref, *rest, rev, final):
    if final:
        of_ref, r_ref, gn_ref, o_ref, st_ref = rest
    else:
        o_ref, st_ref = rest
    L, S = GLA_CHUNK, GLA_SUB

    @pl.when(pl.program_id(1) == 0)
    def _():
        st_ref[...] = jnp.zeros_like(st_ref)

    q = q_ref[...] * (GLA_DK ** -0.5)
    k = k_ref[...]
    v = v_ref[...]
    logits = _dot_hi(sm_ref[...], w2_ref[0]) + b2_ref[0]
    g = (jnp.minimum(logits, 0.0) - jnp.log(1.0 + jnp.exp(-jnp.abs(logits)))) * (1.0 / GLA_GATE_TEMP)
    rr, cc = _iota((L, L), 0), _iota((L, L), 1)
    tri = jnp.where((cc >= rr) if rev else (cc <= rr), 1.0, 0.0)
    b = _dot_hi(tri, g)
    last = 0 if rev else L - 1
    b_last = b[last:last + 1]

    st = st_ref[...]
    o = _dot(q * jnp.exp(b), st, NT)

    head_of_lane = _iota((1, GLA_QK), 1) // GLA_DK
    seg = jnp.where(_iota((GLA_QK, GLA_V), 0) // GLA_DK == _iota((GLA_QK, GLA_V), 1) // GLA_DV, 1.0, 0.0)
    sub_row = _iota((S, 1), 0)
    pieces = []
    for i in range(L // S):
        r0 = S * i
        qi, ki, vi, bi = q[r0:r0 + S], k[r0:r0 + S], v[r0:r0 + S], b[r0:r0 + S]
        acc = jnp.zeros((S, GLA_V), F32)
        if (not rev) and i > 0:
            e0, e1, ref = 0, r0, b[r0 - 1:r0]
        elif rev and i < L // S - 1:
            e0, e1, ref = r0 + S, L, b[r0 + S:r0 + S + 1]
        else:
            e0 = e1 = 0
        if e1 > e0:
            qd = qi * jnp.exp(bi - ref)
            kd = k[e0:e1] * jnp.exp(ref - b[e0:e1])
            ve = v[e0:e1]
            cols = []
            for h in range(GLA_HEADS):
                s_h = _dot(jnp.where(head_of_lane == h, qd, 0.0), kd, NT)
                cols.append(_dot(s_h, ve[:, GLA_DV * h:GLA_DV * (h + 1)]))
            acc = acc + jnp.concatenate(cols, axis=1)
        ps = []
        for s in range(S):
            valid = (sub_row <= s) if rev else (sub_row >= s)
            e = jnp.exp(jnp.where(valid, bi - bi[s:s + 1], NEG))
            ps.append(qi * e * ki[s:s + 1])
        w_ts = _dot(jnp.concatenate(ps, axis=0), seg)
        for s in range(S):
            acc = acc + w_ts[S * s:S * (s + 1)] * vi[s:s + 1]
        pieces.append(acc)
    o = o + jnp.concatenate(pieces, axis=0)

    upd = _dot(v, k * jnp.exp(b_last - b), TN)
    same_head = _iota(st.shape, 0) // GLA_DV == _iota(st.shape, 1) // GLA_DK
    st_ref[...] = jnp.exp(b_last) * st + jnp.where(same_head, upd, 0.0)

    if final:
        tot = o + of_ref[...]
        outs = []
        for h in range(GLA_HEADS):
            oh = tot[:, GLA_DV * h:GLA_DV * (h + 1)]
            outs.append(oh * lax.rsqrt(jnp.mean(oh * oh, axis=-1, keepdims=True) + EPS) * gn_ref[...])
        o_ref[...] = (jnp.concatenate(outs, axis=1) * _silu(r_ref[...])).astype(o_ref.dtype)
    else:
        o_ref[...] = o


def _gla_pass(p, w2, b2, bsz, nt_rows, n_ctx, rev, extra=None):
    nc, nc_ctx = nt_rows // GLA_CHUNK, n_ctx // GLA_CHUNK
    L = GLA_CHUNK

    def rows(width, col):
        return pl.BlockSpec((L, width), lambda b, j: (b * nc + _chunk_order(j, nc_ctx, nc, rev), col))

    d = 1 if rev else 0
    in_specs = [rows(GLA_QK, P_OFF['gla_q'] // GLA_QK), rows(GLA_QK, P_OFF['gla_k'] // GLA_QK),
                rows(GLA_V, P_OFF['gla_v'] // GLA_V), rows(LANE, SMALL_OFF // LANE),
                pl.BlockSpec((1, LANE, GLA_QK), lambda b, j: (d, 0, 0)),
                pl.BlockSpec((1, 1, GLA_QK), lambda b, j: (d, 0, 0))]
    w2_lanes = jnp.zeros((2, LANE, GLA_QK), F32)
    w2_lanes = w2_lanes.at[0, SM_GF:SM_GF + GLA_GATE_RANK].set(w2[0]).at[1, SM_GB:SM_GB + GLA_GATE_RANK].set(w2[1])
    args = [p, p, p, p, w2_lanes, b2.reshape(2, 1, GLA_QK)]
    final = extra is not None
    if final:
        o_first, gn = extra
        in_specs += [rows(GLA_V, 0), rows(GLA_V, P_OFF['gla_r'] // GLA_V),
                     pl.BlockSpec((1, GLA_DV), lambda b, j: (0, 0))]
        args += [o_first, p, gn.reshape(1, GLA_DV)]
    return pl.pallas_call(
        functools.partial(_gla_kernel, rev=rev, final=final),
        grid=(bsz, nc),
        in_specs=in_specs,
        out_specs=rows(GLA_V, 0),
        out_shape=jax.ShapeDtypeStruct((bsz * nt_rows, GLA_V), BF16 if final else F32),
        scratch_shapes=[pltpu.VMEM((GLA_V, GLA_QK), F32)],
        compiler_params=_cparams(2),
        name="gla_bwd_out" if final else "gla_fwd",
    )(*args)


def _att_kernel(sink_ref, q_ref, kp_ref, kc_ref, kn_ref, vp_ref, vc_ref, vn_ref,
                cp_ref, sp_ref, cc_ref, sc_ref, cn_ref, sn_ref, kx_ref, vx_ref, qg_ref, kg_ref,
                o_ref, *, nb_ctx, nb_lat, n_ctx):
    j = pl.program_id(1)
    blk = ATT_BLOCK
    lane = _iota((1, LANE), 1)
    half_mean = jnp.where(_iota((LANE, LANE), 0) // ATT_HD == _iota((LANE, LANE), 1) // ATT_HD,
                          1.0 / ATT_HD, 0.0)

    def norm(x, g_ref):
        return x * lax.rsqrt(_dot(x * x, half_mean) + EPS) * g_ref[...]

    def rope(x, c_ref, s_ref):
        swapped = jnp.where(lane % 32 < 16, pltpu.roll(x, LANE - 16, 1), pltpu.roll(x, 16, 1))
        return x * c_ref[...] + swapped * s_ref[...]

    def attend(qpairs, keys, vals, bias):
        keys_sw = pltpu.roll(keys, ATT_HD, 1).astype(BF16)
        vals_sw = pltpu.roll(vals, ATT_HD, 1).astype(BF16)
        keys, vals = keys.astype(BF16), vals.astype(BF16)
        rep = ATT_Q_HEADS // ATT_KV_HEADS
        outs = []
        for pi in range(ATT_Q_HEADS // 2):
            grp = (2 * pi) // rep
            halves = []
            for par in range(2):
                qm = jnp.where((lane >= ATT_HD) if par else (lane < ATT_HD), qpairs[pi], 0.0)
                s = _dot(qm, keys if par == grp else keys_sw, NT)
                if bias is not None:
                    s = s + bias
                sk = sink_ref[0, 2 * pi + par]
                m = jnp.maximum(jnp.max(s, axis=-1, keepdims=True), sk)
                pr = jnp.exp(s - m)
                den = jnp.sum(pr, axis=-1, keepdims=True) + jnp.exp(sk - m)
                halves.append(_dot(pr, vals if par == grp else vals_sw) / den)
            outs.append(jnp.where(lane < ATT_HD, halves[0], halves[1]))
        o_ref[...] = jnp.concatenate(outs, axis=1).astype(o_ref.dtype)

    kx = norm(kx_ref[...], kg_ref)
    vx = vx_ref[...]

    @pl.when(j < nb_ctx)
    def _():
        qpairs = [norm(q_ref[:, LANE * pi:LANE * (pi + 1)], qg_ref) * (ATT_HD ** -0.5)
                  for pi in range(ATT_Q_HEADS // 2)]
        attend(qpairs, kx, vx, None)

    @pl.when(j >= nb_ctx)
    def _():
        li = j - nb_ctx
        qpairs = [rope(norm(q_ref[:, LANE * pi:LANE * (pi + 1)], qg_ref), cc_ref, sc_ref) * (ATT_HD ** -0.5)
                  for pi in range(ATT_Q_HEADS // 2)]
        keys = jnp.concatenate([kx, rope(norm(kp_ref[...], kg_ref), cp_ref, sp_ref),
                                rope(norm(kc_ref[...], kg_ref), cc_ref, sc_ref),
                                rope(norm(kn_ref[...], kg_ref), cn_ref, sn_ref)], axis=0)
        vals = jnp.concatenate([vx, vp_ref[...], vc_ref[...], vn_ref[...]], axis=0)
        a, jj = _iota((blk, 3 * blk), 0), _iota((blk, 3 * blk), 1)
        ok = (jj >= a) & (jj <= a + 2 * ATT_WINDOW)
        ok = ok & ((jj >= blk) | (li > 0)) & ((jj < 2 * blk) | (li < nb_lat - 1))
        bias = jnp.concatenate([jnp.zeros((blk, n_ctx), F32), jnp.where(ok, 0.0, NEG)], axis=1)
        attend(qpairs, keys, vals, bias)


def _rope_tables(n):
    rows = n // GRID_W
    row = jnp.repeat(jnp.arange(rows), GRID_W).astype(F32)
    col = jnp.tile(jnp.arange(GRID_W), rows).astype(F32)
    axis_dim = ATT_HD // 2
    inv_freq = ROPE_BASE ** (-jnp.arange(0, axis_dim, 2, dtype=F32) / axis_dim)
    ang_r, ang_c = row[:, None] * inv_freq, col[:, None] * inv_freq
    cos = jnp.concatenate([jnp.cos(ang_r), jnp.cos(ang_r), jnp.cos(ang_c), jnp.cos(ang_c)], axis=1)
    sin = jnp.concatenate([-jnp.sin(ang_r), jnp.sin(ang_r), -jnp.sin(ang_c), jnp.sin(ang_c)], axis=1)
    return jnp.tile(cos, (1, LANE // ATT_HD)), jnp.tile(sin, (1, LANE // ATT_HD))


def _attention(p, cos, sin, sink, qg, kg, bsz, nt_rows, n_ctx):
    blk = ATT_BLOCK
    nb, nb_ctx = nt_rows // blk, n_ctx // blk
    nb_lat = nb - nb_ctx
    k_col, v_col = P_OFF['att_k'] // LANE, P_OFF['att_v'] // LANE

    def lat(j, d):
        return jnp.clip(j - nb_ctx + d, 0, nb_lat - 1)

    def kv(col, d):
        return pl.BlockSpec((blk, LANE), lambda b, j: (b * nb + nb_ctx + lat(j, d), col))

    def tab(d):
        return pl.BlockSpec((blk, LANE), lambda b, j: (lat(j, d), 0))

    ctx_rows = nt_rows // n_ctx
    in_specs = [pl.BlockSpec(memory_space=pltpu.SMEM),
                pl.BlockSpec((blk, ATT_Q), lambda b, j: (b * nb + j, P_OFF['att_q'] // ATT_Q)),
                kv(k_col, -1), kv(k_col, 0), kv(k_col, 1), kv(v_col, -1), kv(v_col, 0), kv(v_col, 1),
                tab(-1), tab(-1), tab(0), tab(0), tab(1), tab(1),
                pl.BlockSpec((n_ctx, LANE), lambda b, j: (b * ctx_rows, k_col)),
                pl.BlockSpec((n_ctx, LANE), lambda b, j: (b * ctx_rows, v_col)),
                pl.BlockSpec((1, LANE), lambda b, j: (0, 0)),
                pl.BlockSpec((1, LANE), lambda b, j: (0, 0))]
    tile2 = lambda g: jnp.tile(g.reshape(1, ATT_HD), (1, LANE // ATT_HD))
    return pl.pallas_call(
        functools.partial(_att_kernel, nb_ctx=nb_ctx, nb_lat=nb_lat, n_ctx=n_ctx),
        grid=(bsz, nb),
        in_specs=in_specs,
        out_specs=pl.BlockSpec((blk, ATT_Q), lambda b, j: (b * nb + j, 0)),
        out_shape=jax.ShapeDtypeStruct((bsz * nt_rows, ATT_Q), BF16),
        compiler_params=_cparams(2),
        name="attention",
    )(sink.reshape(1, ATT_Q_HEADS), p, p, p, p, p, p, p, cos, sin, cos, sin, cos, sin, p, p, tile2(qg), tile2(kg))


def _conv_kernel(xp_ref, xc_ref, xn_ref, w_ref, b_ref, o_ref, ext_ref, *, nct, nt):
    j = pl.program_id(1)
    rows = xc_ref.shape[0]
    no_prev = (j == 0) | (j == nct)
    no_next = (j == nct - 1) | (j == nt - 1)
    ext_ref[0:8] = jnp.where(no_prev, 0.0, xp_ref[...])
    ext_ref[8:8 + rows] = xc_ref[...]
    ext_ref[8 + rows:16 + rows] = jnp.where(no_next, 0.0, xn_ref[...])
    pad = SSD_CONV // 2
    acc = b_ref[...] + w_ref[0:1] * ext_ref[pl.ds(8 - pad, rows)]
    for t in range(1, SSD_CONV):
        acc = acc + w_ref[t:t + 1] * ext_ref[pl.ds(8 - pad + t, rows)]
    o_ref[...] = _silu(acc)


def _ssd_conv(p, w, bias, bsz, nt_rows, n_ctx):
    tiles, nct = nt_rows // ROW_TILE, n_ctx // ROW_TILE
    ct = 512
    c0 = P_OFF['ssd_xbc'] // ct
    r8 = ROW_TILE // 8
    last8 = bsz * nt_rows // 8 - 1
    return pl.pallas_call(
        functools.partial(_conv_kernel, nct=nct, nt=tiles),
        grid=(bsz, tiles, SSD_XBC // ct),
        in_specs=[pl.BlockSpec((8, ct), lambda b, j, c: (jnp.maximum((b * tiles + j) * r8 - 1, 0), c0 + c)),
                  pl.BlockSpec((ROW_TILE, ct), lambda b, j, c: (b * tiles + j, c0 + c)),
                  pl.BlockSpec((8, ct), lambda b, j, c: (jnp.minimum((b * tiles + j + 1) * r8, last8), c0 + c)),
                  pl.BlockSpec((SSD_CONV, ct), lambda b, j, c: (0, c)),
                  pl.BlockSpec((1, ct), lambda b, j, c: (0, c))],
        out_specs=pl.BlockSpec((ROW_TILE, ct), lambda b, j, c: (b * tiles + j, c)),
        out_shape=jax.ShapeDtypeStruct((bsz * nt_rows, SSD_XBC), F32),
        scratch_shapes=[pltpu.VMEM((ROW_TILE + 16, ct), F32)],
        compiler_params=_cparams(3),
        name="ssd_conv",
    )(p, p, p, w, bias.reshape(1, -1))


def _ssd_kernel(x_ref, bc_ref, sm_ref, dtb_ref, dtbc_ref, al_ref, alc_ref, *rest, rev, final):
    if final:
        yf_ref, z_ref, d_ref, gn_ref, o_ref, st_ref = rest
    else:
        o_ref, st_ref = rest
    L, P, N = SSD_CHUNK, SSD_HD, SSD_STATE
    gw = SSD_HPG * P

    @pl.when(pl.program_id(1) == 0)
    def _():
        st_ref[...] = jnp.zeros_like(st_ref)

    xs = x_ref[...]
    bm, cm = bc_ref[:, 0:SSD_BC], bc_ref[:, SSD_BC:2 * SSD_BC]
    off = SM_DTB if rev else SM_DTF
    sm = sm_ref[...]
    dt = _softplus(sm + dtb_ref[0])
    a = -dt * jnp.exp(al_ref[0])
    dt_t = _softplus(sm.T + dtbc_ref[0])
    a_t = -dt_t * jnp.exp(alc_ref[0])
    tt, ss = _iota((L, L), 0), _iota((L, L), 1)
    causal = (ss >= tt) if rev else (ss <= tt)
    tri = jnp.where(causal, 1.0, 0.0)
    cum = _dot_hi(tri, a)
    cum_t = _dot_hi(a_t, tri, NT)
    last = 0 if rev else L - 1
    cum_last = cum[last:last + 1]
    expand = jnp.where(_iota((LANE, SSD_INNER), 0) - off == _iota((LANE, SSD_INNER), 1) // P, 1.0, 0.0)
    ecum = _dot_hi(jnp.exp(cum), expand)
    wgt = _dot_hi(jnp.exp(cum_last - cum) * dt, expand)

    st = st_ref[...]
    lane = _iota((1, LANE), 1)
    ys = []
    for g in range(SSD_GROUPS):
        cg, bg = cm[:, N * g:N * (g + 1)], bm[:, N * g:N * (g + 1)]
        cb = _dot(cg, bg, NT)
        for jp in range(SSD_HPG // 2):
            h0 = SSD_HPG * g + 2 * jp
            ms = []
            for h in (off + h0, off + h0 + 1):
                dec = jnp.exp(jnp.where(causal, cum[:, h:h + 1] - cum_t[h:h + 1, :], NEG))
                ms.append(cb * dec * dt_t[h:h + 1, :])
            xp = xs[:, P * h0:P * h0 + LANE]
            rhs = jnp.concatenate([jnp.where(lane < P, xp, 0.0), jnp.where(lane >= P, xp, 0.0)], axis=0)
            ys.append(_dot(jnp.concatenate(ms, axis=1), rhs))
    y = jnp.concatenate(ys, axis=1)
    y = y + jnp.concatenate([_dot(cm[:, N * g:N * (g + 1)], st[:, gw * g:gw * (g + 1)])
                             for g in range(SSD_GROUPS)], axis=1) * ecum
    xw = xs * wgt
    new = jnp.concatenate([_dot(bm[:, N * g:N * (g + 1)], xw[:, gw * g:gw * (g + 1)], TN)
                           for g in range(SSD_GROUPS)], axis=1)
    st_ref[...] = ecum[last:last + 1] * st + new

    if final:
        tot = (y + yf_ref[...] + d_ref[...] * xs) * _silu(z_ref[...])
        outs = []
        for g in range(SSD_GROUPS):
            tg = tot[:, gw * g:gw * (g + 1)]
            outs.append(tg * lax.rsqrt(jnp.mean(tg * tg, axis=-1, keepdims=True) + EPS))
        o_ref[...] = (jnp.concatenate(outs, axis=1) * gn_ref[...]).astype(o_ref.dtype)
    else:
        o_ref[...] = y


def _ssd_pass(p, xbc, dt_bias, a_log, bsz, nt_rows, n_ctx, rev, extra=None):
    L = SSD_CHUNK
    nc, nc_ctx = nt_rows // L, n_ctx // L

    def rows(width, col):
        return pl.BlockSpec((L, width), lambda b, j: (b * nc + _chunk_order(j, nc_ctx, nc, rev), col))

    d = 1 if rev else 0
    vec = pl.BlockSpec((1, 1, LANE), lambda b, j: (d, 0, 0))
    colv = pl.BlockSpec((1, LANE, 1), lambda b, j: (d, 0, 0))
    in_specs = [rows(SSD_INNER, 0), rows(2 * SSD_BC, SSD_INNER // (2 * SSD_BC)), rows(LANE, SMALL_OFF // LANE),
                vec, colv, vec, colv]

    def lanes(v):
        out = jnp.zeros((2, LANE), F32)
        return out.at[0, SM_DTF:SM_DTF + SSD_HEADS].set(v[0]).at[1, SM_DTB:SM_DTB + SSD_HEADS].set(v[1])

    dtb, alog = lanes(dt_bias), lanes(a_log)
    args = [xbc, xbc, p, dtb.reshape(2, 1, LANE), dtb.reshape(2, LANE, 1),
            alog.reshape(2, 1, LANE), alog.reshape(2, LANE, 1)]
    final = extra is not None
    if final:
        y_first, d_skip, gn = extra
        full = pl.BlockSpec((1, SSD_INNER), lambda b, j: (0, 0))
        in_specs += [rows(SSD_INNER, 0), rows(SSD_INNER, P_OFF['ssd_z'] // SSD_INNER), full, full]
        args += [y_first, p, jnp.repeat(d_skip, SSD_HD).reshape(1, SSD_INNER), gn.reshape(1, SSD_INNER)]
    return pl.pallas_call(
        functools.partial(_ssd_kernel, rev=rev, final=final),
        grid=(bsz, nc),
        in_specs=in_specs,
        out_specs=rows(SSD_INNER, 0),
        out_shape=jax.ShapeDtypeStruct((bsz * nt_rows, SSD_INNER), BF16 if final else F32),
        scratch_shapes=[pltpu.VMEM((SSD_STATE, SSD_INNER), F32)],
        compiler_params=_cparams(2),
        name="ssd_bwd_out" if final else "ssd_fwd",
    )(*args)


def _merge_kernel(yg_ref, ya_ref, ys_ref, mg_ref, x_ref, gb_ref, g1_ref, sh2_ref, sc2_ref, n2_ref,
                  wa_ref, wb_ref, wc_ref, wo_ref, xo_ref, h2_ref):
    d = D_MODEL
    gates = _sigmoid(mg_ref[...] + gb_ref[...])
    merged = (gates[:, 0:d] * _dot(yg_ref[...], wa_ref[...])
              + gates[:, d:2 * d] * _dot(ya_ref[...], wb_ref[...])
              + gates[:, 2 * d:3 * d] * _dot(ys_ref[...], wc_ref[...]))
    xn = x_ref[...] + g1_ref[0] * _dot(merged, wo_ref[...])
    xo_ref[...] = xn
    h2_ref[...] = _normmod(xn, n2_ref[...], sh2_ref[0], sc2_ref[0])


def _merge(y_gla, y_att, y_ssd, p, xc, gate_b, mod, norm2_g, wa, wb, wc, wo, bsz, nt_rows, nct):
    tiles = nt_rows // ROW_TILE
    d = D_MODEL

    def rows(width, col=0):
        return pl.BlockSpec((ROW_TILE, width), lambda b, j: (b * tiles + j, col))

    def whole(a):
        return pl.BlockSpec(a.shape, lambda b, j: (0,) * a.ndim)

    gb = gate_b.reshape(1, N_BRANCH * d)
    n2 = norm2_g.reshape(1, d)
    return pl.pallas_call(
        _merge_kernel,
        grid=(bsz, tiles),
        in_specs=[rows(GLA_V), rows(ATT_Q), rows(SSD_INNER), rows(N_BRANCH * d, P_OFF['merge'] // (N_BRANCH * d)),
                  rows(d), whole(gb), _mod_spec(2, nct), _mod_spec(3, nct), _mod_spec(4, nct), whole(n2),
                  whole(wa), whole(wb), whole(wc), whole(wo)],
        out_specs=[rows(d), rows(d)],
        out_shape=[jax.ShapeDtypeStruct(xc.shape, F32), jax.ShapeDtypeStruct(xc.shape, F32)],
        compiler_params=_cparams(2),
        name="merge",
    )(y_gla, y_att, y_ssd, p, xc, gb, mod, mod, mod, n2, wa, wb, wc, wo)


def _router_kernel(h_ref, rw_ref, rb_ref, e_ref, g_ref, r_ref, c_ref, run_ref):
    rows = h_ref.shape[0]

    @pl.when(pl.program_id(0) == 0)
    def _():
        run_ref[...] = jnp.zeros_like(run_ref)

    scores = _sigmoid(_dot_hi(rw_ref[...], h_ref[...], NT))
    biased = scores + rb_ref[...]
    row = lambda a, e: a[e:e + 1, :]
    epg = EXPERTS_PER_GROUP
    best = gidx = None
    for g in range(N_EXPERT_GROUPS):
        vals = [row(biased, epg * g + i) for i in range(epg)]
        top2 = None
        for i in range(epg):
            for i2 in range(i + 1, epg):
                pair = vals[i] + vals[i2]
                top2 = pair if top2 is None else jnp.maximum(top2, pair)
        if g == 0:
            best, gidx = top2, jnp.zeros(top2.shape, jnp.int32)
        else:
            take = top2 > best
            best, gidx = jnp.where(take, top2, best), jnp.where(take, g, gidx)

    def pick(a, i):
        out = row(a, i)
        for g in range(1, N_EXPERT_GROUPS):
            out = jnp.where(gidx == g, row(a, epg * g + i), out)
        return out

    cand = [pick(biased, i) for i in range(epg)]
    raw = [pick(scores, i) for i in range(epg)]

    def argmax_first(vals, exclude):
        bv = bi = bs = None
        for i in range(epg):
            v = vals[i] if exclude is None else jnp.where(exclude == i, -jnp.inf, vals[i])
            if bv is None:
                bv, bi, bs = v, jnp.zeros(v.shape, jnp.int32), raw[0]
            else:
                take = v > bv
                bv, bi, bs = jnp.where(take, v, bv), jnp.where(take, i, bi), jnp.where(take, raw[i], bs)
        return bi, bs

    i0, s0 = argmax_first(cand, None)
    i1, s1 = argmax_first(cand, i0)
    e0, e1 = gidx * epg + i0, gidx * epg + i1
    tot = s0 + s1
    e_ref[...] = jnp.concatenate([e0, e1], axis=0)
    g_ref[...] = jnp.concatenate([s0 / tot, s1 / tot], axis=0)

    eid = _iota((N_EXPERTS, rows), 0)
    oh0 = jnp.where(eid == e0, 1.0, 0.0)
    oh1 = jnp.where(eid == e1, 1.0, 0.0)
    cnt = oh0 + oh1
    before = jnp.where(_iota((rows, rows), 0) < _iota((rows, rows), 1), 1.0, 0.0)
    prefix = _dot(cnt, before) + run_ref[:, 0:1]
    r0 = jnp.sum(oh0 * prefix, axis=0, keepdims=True)
    r1 = jnp.sum(oh1 * prefix, axis=0, keepdims=True)
    r_ref[...] = jnp.concatenate([r0, r1], axis=0).astype(jnp.int32)
    run_ref[...] = run_ref[...] + jnp.sum(cnt, axis=1, keepdims=True)
    c_ref[...] = run_ref[...].astype(jnp.int32)


def _router(h2, router_w, router_b):
    t = h2.shape[0]
    tiles = t // ROW_TILE
    lanes = pl.BlockSpec((2, ROW_TILE), lambda i: (0, i))
    return pl.pallas_call(
        _router_kernel,
        grid=(tiles,),
        in_specs=[pl.BlockSpec((ROW_TILE, D_MODEL), lambda i: (i, 0)),
                  pl.BlockSpec((N_EXPERTS, D_MODEL), lambda i: (0, 0)),
                  pl.BlockSpec((N_EXPERTS, 1), lambda i: (0, 0))],
        out_specs=[lanes, lanes, lanes, pl.BlockSpec((N_EXPERTS, LANE), lambda i: (0, 0))],
        out_shape=[jax.ShapeDtypeStruct((2, t), jnp.int32), jax.ShapeDtypeStruct((2, t), F32),
                   jax.ShapeDtypeStruct((2, t), jnp.int32), jax.ShapeDtypeStruct((N_EXPERTS, LANE), jnp.int32)],
        scratch_shapes=[pltpu.VMEM((N_EXPERTS, LANE), F32)],
        compiler_params=_cparams(1),
        name="router",
    )(h2, router_w.T, router_b.reshape(N_EXPERTS, 1))


def _dispatch_kernel(dest_ref, h_ref, buf_in_ref, buf_ref, sem):
    del buf_in_ref
    rows = h_ref.shape[0]

    def row_copy(r, k):
        return pltpu.make_async_copy(h_ref.at[pl.ds(r, 1)], buf_ref.at[pl.ds(dest_ref[k, r], 1)], sem)

    def start(r, carry):
        row_copy(r, 0).start()
        row_copy(r, 1).start()
        return carry

    def wait(r, carry):
        row_copy(r, 0).wait()
        row_copy(r, 1).wait()
        return carry

    lax.fori_loop(0, rows, start, 0)
    lax.fori_loop(0, rows, wait, 0)


def _dispatch(h2, dest, n_slots):
    t, d = h2.shape
    return pl.pallas_call(
        _dispatch_kernel,
        grid=(t // ROW_TILE,),
        in_specs=[pl.BlockSpec((2, ROW_TILE), lambda i: (0, i), memory_space=pltpu.SMEM),
                  pl.BlockSpec((ROW_TILE, d), lambda i: (i, 0)),
                  pl.BlockSpec(memory_space=pl.ANY)],
        out_specs=pl.BlockSpec(memory_space=pl.ANY),
        out_shape=jax.ShapeDtypeStruct((n_slots, d), F32),
        scratch_shapes=[pltpu.SemaphoreType.DMA(())],
        input_output_aliases={2: 0},
        compiler_params=_cparams(1),
        name="moe_dispatch",
    )(dest, h2, jnp.zeros((n_slots, d), F32))


def _expert_kernel(be_ref, nu_ref, x_ref, w1_ref, w3_ref, w2_ref, o_ref):
    del be_ref
    used = pl.program_id(0) < nu_ref[0]

    @pl.when(used)
    def _():
        x = x_ref[...].astype(BF16)
        hid = _silu(_dot(x, w1_ref[0])) * _dot(x, w3_ref[0])
        o_ref[...] = _dot(hid, w2_ref[0])

    @pl.when(jnp.logical_not(used))
    def _():
        o_ref[...] = jnp.zeros_like(o_ref)


def _experts(buf, block_e, n_used, w1, w3, w2):
    n_slots, d = buf.shape
    ff = w1.shape[-1]
    grid_spec = pltpu.PrefetchScalarGridSpec(
        num_scalar_prefetch=2,
        grid=(n_slots // MOE_BLOCK,),
        in_specs=[pl.BlockSpec((MOE_BLOCK, d), lambda i, be, nu: (i, 0)),
                  pl.BlockSpec((1, d, ff), lambda i, be, nu: (be[i], 0, 0)),
                  pl.BlockSpec((1, d, ff), lambda i, be, nu: (be[i], 0, 0)),
                  pl.BlockSpec((1, ff, d), lambda i, be, nu: (be[i], 0, 0))],
        out_specs=pl.BlockSpec((MOE_BLOCK, d), lambda i, be, nu: (i, 0)),
    )
    return pl.pallas_call(
        _expert_kernel,
        grid_spec=grid_spec,
        out_shape=jax.ShapeDtypeStruct((n_slots, d), F32),
        compiler_params=_cparams(1),
        name="moe_experts",
    )(block_e, n_used, buf, w1, w3, w2)


def _combine_kernel(dest_ref, g_ref, x_ref, g2_ref, buf_ref, o_ref, rows_ref, sem):
    rows = x_ref.shape[0]

    def row_copy(r, k):
        return pltpu.make_async_copy(buf_ref.at[pl.ds(dest_ref[k, r], 1)], rows_ref.at[k, pl.ds(r, 1)], sem)

    def start(r, carry):
        row_copy(r, 0).start()
        row_copy(r, 1).start()
        return carry

    def wait(r, carry):
        row_copy(r, 0).wait()
        row_copy(r, 1).wait()
        return carry

    lax.fori_loop(0, rows, start, 0)
    lax.fori_loop(0, rows, wait, 0)
    y = g_ref[:, 0:1] * rows_ref[0] + g_ref[:, 1:2] * rows_ref[1]
    o_ref[...] = x_ref[...] + g2_ref[0] * y


def _combine(out_buf, dest, gates_col, x_new, mod, bsz, nt_rows, nct, latent_only):
    tiles = nt_rows // ROW_TILE
    d = D_MODEL
    skip = nct if latent_only else 0
    out_tiles = tiles - skip
    src = lambda b, j: b * tiles + skip + j
    return pl.pallas_call(
        _combine_kernel,
        grid=(bsz, out_tiles),
        in_specs=[pl.BlockSpec((2, ROW_TILE), lambda b, j: (0, src(b, j)), memory_space=pltpu.SMEM),
                  pl.BlockSpec((ROW_TILE, 2), lambda b, j: (src(b, j), 0)),
                  pl.BlockSpec((ROW_TILE, d), lambda b, j: (src(b, j), 0)),
                  pl.BlockSpec((1, 1, d), lambda b, j: (2 * b + jnp.where(skip + j >= nct, 1, 0), 0, 5)),
                  pl.BlockSpec(memory_space=pl.ANY)],
        out_specs=pl.BlockSpec((ROW_TILE, d), lambda b, j: (b * out_tiles + j, 0)),
        out_shape=jax.ShapeDtypeStruct((bsz * out_tiles * ROW_TILE, d), F32),
        scratch_shapes=[pltpu.VMEM((2, ROW_TILE, d), F32), pltpu.SemaphoreType.DMA(())],
        compiler_params=_cparams(2),
        name="moe_combine",
    )(dest, gates_col, x_new, mod, out_buf)


def _moe(h2, x_new, mod, router_w, router_b, w1, w3, w2, bsz, nt_rows, nct, latent_only):
    t = h2.shape[0]
    e_idx, gates, rank, counts = _router(h2, router_w, router_b)
    counts = counts[:, 0]
    padded = (counts + MOE_BLOCK - 1) // MOE_BLOCK * MOE_BLOCK
    pad_end = jnp.cumsum(padded)
    pad_start = pad_end - padded
    dest = pad_start[e_idx] + rank
    n_blocks = -(-2 * t // MOE_BLOCK) + N_EXPERTS
    block_e = jnp.minimum(jnp.searchsorted(pad_end, jnp.arange(n_blocks, dtype=jnp.int32) * MOE_BLOCK,
                                           side='right'), N_EXPERTS - 1).astype(jnp.int32)
    n_used = (pad_end[-1:] // MOE_BLOCK).astype(jnp.int32)
    buf = _dispatch(h2, dest, n_blocks * MOE_BLOCK)
    out_buf = _experts(buf, block_e, n_used, w1, w3, w2)
    return _combine(out_buf, dest, gates.T, x_new, mod, bsz, nt_rows, nct, latent_only)


def _permute_w_in(w_in):
    offsets = np.concatenate([[0], np.cumsum(IN_SPLITS)])
    start = dict(zip(IN_NAMES, offsets[:-1]))
    width = dict(zip(IN_NAMES, IN_SPLITS))
    cols = [w_in[:, start[n]:start[n] + width[n]] for n in P_ORDER]
    cols.append(jnp.zeros((w_in.shape[0], SMALL_PAD), w_in.dtype))
    return jnp.concatenate(cols, axis=1).astype(BF16)


def _layer(xc, mod, lp, router_w, router_b, rope, bsz, nt_rows, n_ctx, last):
    nct = n_ctx // ROW_TILE
    h = _norm_modulate(xc, lp['norm1_g'], mod, bsz, nt_rows, nct)
    p = _in_proj(h, _permute_w_in(lp['w_in']))

    o_f = _gla_pass(p, lp['gla_w2'], lp['gla_b2'], bsz, nt_rows, n_ctx, rev=False)
    y_gla = _gla_pass(p, lp['gla_w2'], lp['gla_b2'], bsz, nt_rows, n_ctx, rev=True,
                      extra=(o_f, lp['gla_norm_g']))
    y_att = _attention(p, rope[0], rope[1], lp['att_sink'], lp['q_norm_g'], lp['k_norm_g'], bsz, nt_rows, n_ctx)
    xbc = _ssd_conv(p, lp['ssd_conv_w'], lp['ssd_conv_b'], bsz, nt_rows, n_ctx)
    y_f = _ssd_pass(p, xbc, lp['ssd_dt_bias'], lp['ssd_a_log'], bsz, nt_rows, n_ctx, rev=False)
    y_ssd = _ssd_pass(p, xbc, lp['ssd_dt_bias'], lp['ssd_a_log'], bsz, nt_rows, n_ctx, rev=True,
                      extra=(y_f, lp['ssd_d'], lp['ssd_norm_g']))

    x_new, h2 = _merge(y_gla, y_att, y_ssd, p, xc, lp['gate_b'], mod, lp['norm2_g'],
                       lp['w_br_a'].astype(BF16), lp['w_br_b'].astype(BF16), lp['w_br_c'].astype(BF16),
                       lp['w_out'].astype(BF16), bsz, nt_rows, nct)
    return _moe(h2, x_new, mod, router_w, router_b, lp['w1'].astype(BF16), lp['w3'].astype(BF16),
                lp['w2'].astype(BF16), bsz, nt_rows, nct, latent_only=last)


def kernel(x, c, ctx, c_ctx, w_mod, b_mod, norm1_g, w_in, gla_w2, gla_b2, gla_norm_g, q_norm_g, k_norm_g, att_sink, ssd_conv_w, ssd_conv_b, ssd_dt_bias, ssd_a_log, ssd_d, ssd_norm_g, gate_b, w_br_a, w_br_b, w_br_c, w_out, norm2_g, router_w, router_b, w1, w3, w2):
    bsz, seq, d = x.shape
    n_ctx = ctx.shape[1]
    nt_rows = n_ctx + seq
    assert d == D_MODEL and n_ctx % ROW_TILE == 0 and seq % ROW_TILE == 0 and nt_rows % n_ctx == 0
    assert seq % GRID_W == 0 and bsz < 16

    cc = jnp.zeros((16, d), F32).at[:bsz].set(c).at[bsz].set(c_ctx)
    mods = _mod_vectors(cc, w_mod, b_mod)
    rope = _rope_tables(seq)
    xc = jnp.concatenate([ctx, x], axis=1).reshape(bsz * nt_rows, d)
    params = dict(norm1_g=norm1_g, w_in=w_in, gla_w2=gla_w2, gla_b2=gla_b2, gla_norm_g=gla_norm_g,
                  q_norm_g=q_norm_g, k_norm_g=k_norm_g, att_sink=att_sink, ssd_conv_w=ssd_conv_w,
                  ssd_conv_b=ssd_conv_b, ssd_dt_bias=ssd_dt_bias, ssd_a_log=ssd_a_log, ssd_d=ssd_d,
                  ssd_norm_g=ssd_norm_g, gate_b=gate_b, w_br_a=w_br_a, w_br_b=w_br_b, w_br_c=w_br_c,
                  w_out=w_out, norm2_g=norm2_g, w1=w1, w3=w3, w2=w2)
    for l in range(DEPTH):
        lp = {name: val[l] for name, val in params.items()}
        m = mods[l]
        mod = jnp.stack([jnp.broadcast_to(m[bsz], (bsz, N_MOD * d)), m[:bsz]], axis=1)
        mod = mod.reshape(2 * bsz, 1, N_MOD * d)
        xc = _layer(xc, mod, lp, router_w, router_b, rope, bsz, nt_rows, n_ctx, last=(l == DEPTH - 1))
    return xc.reshape(bsz, seq, d)
```

```python
import functools

import numpy as np
import jax
import jax.numpy as jnp
from jax import lax
from jax.experimental import pallas as pl
from jax.experimental.pallas import tpu as pltpu

F32 = jnp.float32
BF16 = jnp.bfloat16

D_MODEL = 1024
DEPTH = 2
GRID_W = 64
EPS = 1e-6
N_MOD = 6

GLA_HEADS = 4
GLA_DK = 64
GLA_DV = 128
GLA_GATE_RANK = 16
GLA_GATE_TEMP = 16.0
GLA_CHUNK = 64
GLA_SUB = 8
GLA_STEP = 4
GLA_QK = GLA_HEADS * GLA_DK
GLA_V = GLA_HEADS * GLA_DV

ATT_Q_HEADS = 8
ATT_KV_HEADS = 2
ATT_HD = 64
ATT_WINDOW = 128
ATT_BLOCK = 128
ROPE_BASE = 10000.0
ATT_Q = ATT_Q_HEADS * ATT_HD
ATT_KV = ATT_KV_HEADS * ATT_HD

SSD_HEADS = 16
SSD_HD = 64
SSD_GROUPS = 2
SSD_STATE = 128
SSD_CONV = 5
SSD_CHUNK = 128
SSD_INNER = SSD_HEADS * SSD_HD
SSD_BC = SSD_GROUPS * SSD_STATE
SSD_XBC = SSD_INNER + 2 * SSD_BC
SSD_HPG = SSD_HEADS // SSD_GROUPS

N_BRANCH = 3
N_EXPERTS = 16
N_EXPERT_GROUPS = 4
EXPERTS_PER_GROUP = N_EXPERTS // N_EXPERT_GROUPS
EXPERT_FF = 1024

IN_NAMES = ('gla_q', 'gla_k', 'gla_v', 'gla_r', 'gla_gf', 'gla_gb', 'att_q', 'att_k', 'att_v',
            'ssd_z', 'ssd_xbc', 'ssd_dtf', 'ssd_dtb', 'merge')
IN_SPLITS = (GLA_QK, GLA_QK, GLA_V, GLA_V, GLA_GATE_RANK, GLA_GATE_RANK, ATT_Q, ATT_KV, ATT_KV,
             SSD_INNER, SSD_XBC, SSD_HEADS, SSD_HEADS, N_BRANCH * D_MODEL)

P_ORDER = ('merge', 'ssd_z', 'ssd_xbc', 'gla_v', 'gla_r', 'att_q', 'gla_q', 'gla_k', 'att_k', 'att_v',
           'gla_gf', 'gla_gb', 'ssd_dtf', 'ssd_dtb')
LANE = 128
SMALL_PAD = 4 * 16
N_P = sum(IN_SPLITS) + SMALL_PAD


def _p_offsets():
    width = dict(zip(IN_NAMES, IN_SPLITS))
    off, o = {}, 0
    for name in P_ORDER:
        off[name] = o
        o += width[name]
    return off


P_OFF = _p_offsets()
SMALL_OFF = P_OFF['gla_gf']
SM_GF, SM_GB, SM_DTF, SM_DTB = 0, 16, 32, 48

ROW_TILE = 256
MOE_BLOCK = 256
NEG = -1e30
DMA_UNROLL = 8
VMEM_LIMIT = 56 * 1024 * 1024

NN = (((1,), (0,)), ((), ()))
NT = (((1,), (1,)), ((), ()))
TN = (((0,), (0,)), ((), ()))


def _dot(a, b, dims=NN):
    return lax.dot_general(a.astype(BF16), b.astype(BF16), dims, preferred_element_type=F32)


def _dot_hi(a, b, dims=NN):
    return lax.dot_general(a.astype(F32), b.astype(F32), dims, precision=lax.Precision.HIGHEST,
                           preferred_element_type=F32)


def _split(x, terms):
    parts, rem = [], x
    for i in range(terms):
        part = rem.astype(BF16)
        parts.append(part)
        if i + 1 < terms:
            rem = rem - part.astype(F32)
    return parts


def _dot_exact_rhs(m01, x, terms, dims=NN):
    m01 = m01.astype(BF16)
    return sum(lax.dot_general(m01, part, dims, preferred_element_type=F32) for part in _split(x, terms))


def _dot_exact_lhs(x, m01, terms, dims=NN):
    m01 = m01.astype(BF16)
    return sum(lax.dot_general(part, m01, dims, preferred_element_type=F32) for part in _split(x, terms))


def _dot3(a, b, dims=NN):
    (ah, al), (bh, bl) = _split(a, 2), _split(b, 2)
    mm = lambda x, y: lax.dot_general(x, y, dims, preferred_element_type=F32)
    return mm(ah, bh) + mm(ah, bl) + mm(al, bh)


def _sigmoid(x):
    return 1.0 / (1.0 + jnp.exp(-x))


def _silu(x):
    return x * _sigmoid(x)


def _softplus(x):
    return jnp.maximum(x, 0.0) + jnp.log(1.0 + jnp.exp(-jnp.abs(x)))


def _iota(shape, dim):
    return lax.broadcasted_iota(jnp.int32, shape, dim)


def _cparams(n_axes):
    return pltpu.CompilerParams(dimension_semantics=("arbitrary",) * n_axes, vmem_limit_bytes=VMEM_LIMIT)


def _largest_tile(n, cap, mult):
    t = (min(cap, n) // mult) * mult
    while n % t:
        t -= mult
    return t


def _mod_kernel(c_ref, w_ref, b_ref, o_ref):
    o_ref[0] = _dot_hi(_silu(c_ref[...]), w_ref[0]) + b_ref[0]


def _mod_vectors(cc, w_mod, b_mod):
    n_l, d, n6 = w_mod.shape
    tn = 1024
    return pl.pallas_call(
        _mod_kernel,
        grid=(n_l, n6 // tn),
        in_specs=[pl.BlockSpec(cc.shape, lambda l, j: (0, 0)),
                  pl.BlockSpec((1, d, tn), lambda l, j: (l, 0, j)),
                  pl.BlockSpec((1, 1, tn), lambda l, j: (l, 0, j))],
        out_specs=pl.BlockSpec((1, cc.shape[0], tn), lambda l, j: (l, 0, j)),
        out_shape=jax.ShapeDtypeStruct((n_l, cc.shape[0], n6), F32),
        compiler_params=_cparams(2),
        name="mod_vectors",
    )(cc, w_mod, b_mod.reshape(n_l, 1, n6))


def _mod_spec(which, nct):
    return pl.BlockSpec((1, 1, D_MODEL), lambda b, j: (2 * b + jnp.where(j >= nct, 1, 0), 0, which))


def _normmod(x, g, shift, scale):
    y = x * lax.rsqrt(jnp.mean(x * x, axis=-1, keepdims=True) + EPS) * g
    return y * (1.0 + scale) + shift


def _normmod_kernel(x_ref, g_ref, sh_ref, sc_ref, o_ref):
    o_ref[...] = _normmod(x_ref[...], g_ref[...], sh_ref[0], sc_ref[0]).astype(o_ref.dtype)


def _norm_modulate(xc, g, mod, bsz, nt_rows, nct):
    tiles = nt_rows // ROW_TILE
    return pl.pallas_call(
        _normmod_kernel,
        grid=(bsz, tiles),
        in_specs=[pl.BlockSpec((ROW_TILE, D_MODEL), lambda b, j: (b * tiles + j, 0)),
                  pl.BlockSpec((1, D_MODEL), lambda b, j: (0, 0)),
                  _mod_spec(0, nct), _mod_spec(1, nct)],
        out_specs=pl.BlockSpec((ROW_TILE, D_MODEL), lambda b, j: (b * tiles + j, 0)),
        out_shape=jax.ShapeDtypeStruct(xc.shape, BF16),
        compiler_params=_cparams(2),
        name="norm_modulate",
    )(xc, g.reshape(1, -1), mod, mod)


def _matmul_kernel(a_ref, w_ref, o_ref):
    o_ref[...] = jnp.dot(a_ref[...], w_ref[...], preferred_element_type=F32).astype(o_ref.dtype)


def _in_proj(h, w):
    m, kdim = h.shape
    n = w.shape[1]
    tm = _largest_tile(m, 2048, 256)
    tn = 1152
    return pl.pallas_call(
        _matmul_kernel,
        grid=(m // tm, n // tn),
        in_specs=[pl.BlockSpec((tm, kdim), lambda i, j: (i, 0)),
                  pl.BlockSpec((kdim, tn), lambda i, j: (0, j))],
        out_specs=pl.BlockSpec((tm, tn), lambda i, j: (i, j)),
        out_shape=jax.ShapeDtypeStruct((m, n), F32),
        compiler_params=_cparams(2),
        name="in_proj",
    )(h, w)


def _chunk_order(j, nc_ctx, nc, rev):
    if not rev:
        return j
    return jnp.where(j < nc_ctx, nc_ctx - 1 - j, nc - 1 - (j - nc_ctx))


def _gla_kernel(q_ref, k_ref, v_ref, sm_ref, w2_ref, b2_ref, *rest, rev, final):
    if final:
        of_ref, r_ref, gn_ref, o_ref, st_ref, b_ref = rest
    else:
        o_ref, st_ref, b_ref = rest
    L, S = GLA_CHUNK, GLA_SUB
    rows = q_ref.shape[0]

    @pl.when(pl.program_id(1) == 0)
    def _():
        st_ref[...] = jnp.zeros_like(st_ref)

    logits = _dot3(sm_ref[...], w2_ref[0]) + b2_ref[0]
    g = (jnp.minimum(logits, 0.0) - jnp.log(1.0 + jnp.exp(-jnp.abs(logits)))) * (1.0 / GLA_GATE_TEMP)
    rr, cc = _iota((rows, rows), 0), _iota((rows, rows), 1)
    tri = jnp.where((rr // L == cc // L) & ((cc >= rr) if rev else (cc <= rr)), 1.0, 0.0)
    b_ref[...] = _dot_exact_rhs(tri, g, 3)

    head_of_lane = _iota((1, GLA_QK), 1) // GLA_DK
    seg = jnp.where(_iota((GLA_QK, GLA_V), 0) // GLA_DK == _iota((GLA_QK, GLA_V), 1) // GLA_DV, 1.0, 0.0)
    sub_row = _iota((S, 1), 0)
    by_head = lambda x: [jnp.where(head_of_lane == h, x, 0.0) for h in range(GLA_HEADS)]
    n_sub = L // S
    order = list(reversed(range(rows // L))) if rev else list(range(rows // L))
    last = 0 if rev else L - 1

    groups = []
    for i in range(L // (2 * S)):
        r0 = 2 * S * i
        if (not rev) and i > 0:
            groups.append((r0, 2 * S, 0, r0, r0 - 1))
        if rev and r0 + 2 * S < L:
            groups.append((r0, 2 * S, r0 + 2 * S, L - r0 - 2 * S, r0 + 2 * S))
        groups.append((r0, S, r0 + S, S, r0 + S) if rev else (r0 + S, S, r0, S, r0 + S - 1))

    vals = {c: (q_ref[L * c:L * (c + 1)] * (GLA_DK ** -0.5), k_ref[L * c:L * (c + 1)],
                v_ref[L * c:L * (c + 1)], b_ref[L * c:L * (c + 1)]) for c in order}

    scores = {}
    for c in order:
        q, k, v, b = vals[c]
        for gi, (q0, nq, k0, nk, edge) in enumerate(groups):
            ref = b[edge:edge + 1]
            qd = q[q0:q0 + nq] * jnp.exp(b[q0:q0 + nq] - ref)
            kd = k[k0:k0 + nk] * jnp.exp(ref - b[k0:k0 + nk])
            scores[c, gi] = _dot(jnp.concatenate(by_head(qd), axis=0), kd, NT)

    ps = []
    for c in order:
        q, k, v, b = vals[c]
        for i in range(n_sub):
            qi, ki, bi = q[S * i:S * (i + 1)], k[S * i:S * (i + 1)], b[S * i:S * (i + 1)]
            for s in range(S):
                valid = (sub_row <= s) if rev else (sub_row >= s)
                ps.append(qi * jnp.exp(jnp.where(valid, bi - bi[s:s + 1], NEG)) * ki[s:s + 1])
    w_all = _dot(jnp.concatenate(ps, axis=0), seg)

    intra = {}
    for ci, c in enumerate(order):
        q, k, v, b = vals[c]
        pieces = [None] * n_sub

        def add(i, val):
            pieces[i] = val if pieces[i] is None else pieces[i] + val

        for gi, (q0, nq, k0, nk, edge) in enumerate(groups):
            s_g = scores[c, gi]
            val = jnp.concatenate([_dot(s_g[nq * h:nq * (h + 1)], v[k0:k0 + nk, GLA_DV * h:GLA_DV * (h + 1)])
                                   for h in range(GLA_HEADS)], axis=1)
            for j in range(nq // S):
                add(q0 // S + j, val[S * j:S * (j + 1)])
        for i in range(n_sub):
            vi = v[S * i:S * (i + 1)]
            base = (ci * n_sub + i) * S * S
            for s in range(S):
                add(i, w_all[base + S * s:base + S * (s + 1)] * vi[s:s + 1])
        intra[c] = jnp.concatenate(pieces, axis=0)

    st = st_ref[...]
    for c in order:
        q, k, v, b = vals[c]
        b_last = b[last:last + 1]
        o = intra[c] + jnp.concatenate([_dot(qh, st, NT) for qh in by_head(q * jnp.exp(b))], axis=1)
        kk = k * jnp.exp(b_last - b)
        new = None
        for h in range(GLA_HEADS):
            u = _dot(v[:, GLA_DV * h:GLA_DV * (h + 1)], kk, TN)
            new = u if new is None else jnp.where(head_of_lane == h, u, new)
        st = jnp.exp(b_last) * st + new
        if final:
            tot = o + of_ref[L * c:L * (c + 1)]
            outs = []
            for h in range(GLA_HEADS):
                oh = tot[:, GLA_DV * h:GLA_DV * (h + 1)]
                outs.append(oh * lax.rsqrt(jnp.mean(oh * oh, axis=-1, keepdims=True) + EPS) * gn_ref[...])
            o_ref[L * c:L * (c + 1)] = (jnp.concatenate(outs, axis=1) * _silu(r_ref[L * c:L * (c + 1)])).astype(o_ref.dtype)
        else:
            o_ref[L * c:L * (c + 1)] = o
    st_ref[...] = st


def _gla_pass(p, w2, b2, bsz, nt_rows, n_ctx, rev, extra=None):
    rows_per_step = GLA_STEP * GLA_CHUNK
    nb, nb_ctx = nt_rows // rows_per_step, n_ctx // rows_per_step

    def rows(width, col):
        return pl.BlockSpec((rows_per_step, width), lambda b, j: (b * nb + _chunk_order(j, nb_ctx, nb, rev), col))

    d = 1 if rev else 0
    in_specs = [rows(GLA_QK, P_OFF['gla_q'] // GLA_QK), rows(GLA_QK, P_OFF['gla_k'] // GLA_QK),
                rows(GLA_V, P_OFF['gla_v'] // GLA_V), rows(LANE, SMALL_OFF // LANE),
                pl.BlockSpec((1, LANE, GLA_QK), lambda b, j: (d, 0, 0)),
                pl.BlockSpec((1, 1, GLA_QK), lambda b, j: (d, 0, 0))]
    w2_lanes = jnp.zeros((2, LANE, GLA_QK), F32)
    w2_lanes = w2_lanes.at[0, SM_GF:SM_GF + GLA_GATE_RANK].set(w2[0]).at[1, SM_GB:SM_GB + GLA_GATE_RANK].set(w2[1])
    args = [p, p, p, p, w2_lanes, b2.reshape(2, 1, GLA_QK)]
    final = extra is not None
    if final:
        o_first, gn = extra
        in_specs += [rows(GLA_V, 0), rows(GLA_V, P_OFF['gla_r'] // GLA_V),
                     pl.BlockSpec((1, GLA_DV), lambda b, j: (0, 0))]
        args += [o_first, p, gn.reshape(1, GLA_DV)]
    return pl.pallas_call(
        functools.partial(_gla_kernel, rev=rev, final=final),
        grid=(bsz, nb),
        in_specs=in_specs,
        out_specs=rows(GLA_V, 0),
        out_shape=jax.ShapeDtypeStruct((bsz * nt_rows, GLA_V), BF16 if final else F32),
        scratch_shapes=[pltpu.VMEM((GLA_DV, GLA_QK), F32), pltpu.VMEM((rows_per_step, GLA_QK), F32)],
        compiler_params=_cparams(2),
        name="gla_bwd_out" if final else "gla_fwd",
    )(*args)


def _att_kernel(sink_ref, q_ref, kp_ref, kc_ref, kn_ref, vp_ref, vc_ref, vn_ref,
                cp_ref, sp_ref, cc_ref, sc_ref, cn_ref, sn_ref, kx_ref, vx_ref, qg_ref, kg_ref,
                o_ref, *, nb_ctx, nb_lat, n_ctx):
    j = pl.program_id(1)
    blk = ATT_BLOCK
    lane = _iota((1, LANE), 1)
    half_mean = jnp.where(_iota((LANE, LANE), 0) // ATT_HD == _iota((LANE, LANE), 1) // ATT_HD,
                          1.0 / ATT_HD, 0.0)

    def norm(x, g_ref):
        return x * lax.rsqrt(_dot(x * x, half_mean) + EPS) * g_ref[...]

    def rope(x, c_ref, s_ref):
        swapped = jnp.where(lane % 32 < 16, pltpu.roll(x, LANE - 16, 1), pltpu.roll(x, 16, 1))
        return x * c_ref[...] + swapped * s_ref[...]

    def attend(qpairs, keys, vals, bias):
        keys_sw = pltpu.roll(keys, ATT_HD, 1).astype(BF16)
        vals_sw = pltpu.roll(vals, ATT_HD, 1).astype(BF16)
        keys, vals = keys.astype(BF16), vals.astype(BF16)
        rep = ATT_Q_HEADS // ATT_KV_HEADS
        heads = range(ATT_Q_HEADS)
        straight = [(h % 2) == (h // rep) for h in heads]
        scores = []
        for h in heads:
            qm = jnp.where((lane >= ATT_HD) if h % 2 else (lane < ATT_HD), qpairs[h // 2], 0.0)
            s = _dot(qm, keys if straight[h] else keys_sw, NT)
            scores.append(s if bias is None else s + bias)
        probs, dens = [], []
        for h in heads:
            sk = sink_ref[0, h]
            m = jnp.maximum(jnp.max(scores[h], axis=-1, keepdims=True), sk)
            pr = jnp.exp(scores[h] - m)
            probs.append(pr)
            dens.append(jnp.sum(pr, axis=-1, keepdims=True) + jnp.exp(sk - m))
        outs = [_dot(probs[h], vals if straight[h] else vals_sw) / dens[h] for h in heads]
        o_ref[...] = jnp.concatenate([jnp.where(lane < ATT_HD, outs[2 * pi], outs[2 * pi + 1])
                                      for pi in range(ATT_Q_HEADS // 2)], axis=1).astype(o_ref.dtype)

    kx = norm(kx_ref[...], kg_ref)
    vx = vx_ref[...]

    @pl.when(j < nb_ctx)
    def _():
        qpairs = [norm(q_ref[:, LANE * pi:LANE * (pi + 1)], qg_ref) * (ATT_HD ** -0.5)
                  for pi in range(ATT_Q_HEADS // 2)]
        attend(qpairs, kx, vx, None)

    @pl.when(j >= nb_ctx)
    def _():
        li = j - nb_ctx
        qpairs = [rope(norm(q_ref[:, LANE * pi:LANE * (pi + 1)], qg_ref), cc_ref, sc_ref) * (ATT_HD ** -0.5)
                  for pi in range(ATT_Q_HEADS // 2)]
        keys = jnp.concatenate([kx, rope(norm(kp_ref[...], kg_ref), cp_ref, sp_ref),
                                rope(norm(kc_ref[...], kg_ref), cc_ref, sc_ref),
                                rope(norm(kn_ref[...], kg_ref), cn_ref, sn_ref)], axis=0)
        vals = jnp.concatenate([vx, vp_ref[...], vc_ref[...], vn_ref[...]], axis=0)
        a, jj = _iota((blk, 3 * blk), 0), _iota((blk, 3 * blk), 1)
        ok = (jj >= a) & (jj <= a + 2 * ATT_WINDOW)
        ok = ok & ((jj >= blk) | (li > 0)) & ((jj < 2 * blk) | (li < nb_lat - 1))
        bias = jnp.concatenate([jnp.zeros((blk, n_ctx), F32), jnp.where(ok, 0.0, NEG)], axis=1)
        attend(qpairs, keys, vals, bias)


def _rope_tables(n):
    rows = n // GRID_W
    row = jnp.repeat(jnp.arange(rows), GRID_W).astype(F32)
    col = jnp.tile(jnp.arange(GRID_W), rows).astype(F32)
    axis_dim = ATT_HD // 2
    inv_freq = ROPE_BASE ** (-jnp.arange(0, axis_dim, 2, dtype=F32) / axis_dim)
    ang_r, ang_c = row[:, None] * inv_freq, col[:, None] * inv_freq
    cos = jnp.concatenate([jnp.cos(ang_r), jnp.cos(ang_r), jnp.cos(ang_c), jnp.cos(ang_c)], axis=1)
    sin = jnp.concatenate([-jnp.sin(ang_r), jnp.sin(ang_r), -jnp.sin(ang_c), jnp.sin(ang_c)], axis=1)
    return jnp.tile(cos, (1, LANE // ATT_HD)), jnp.tile(sin, (1, LANE // ATT_HD))


def _attention(p, cos, sin, sink, qg, kg, bsz, nt_rows, n_ctx):
    blk = ATT_BLOCK
    nb, nb_ctx = nt_rows // blk, n_ctx // blk
    nb_lat = nb - nb_ctx
    k_col, v_col = P_OFF['att_k'] // LANE, P_OFF['att_v'] // LANE

    def lat(j, d):
        return jnp.clip(j - nb_ctx + d, 0, nb_lat - 1)

    def kv(col, d):
        return pl.BlockSpec((blk, LANE), lambda b, j: (b * nb + nb_ctx + lat(j, d), col))

    def tab(d):
        return pl.BlockSpec((blk, LANE), lambda b, j: (lat(j, d), 0))

    ctx_rows = nt_rows // n_ctx
    in_specs = [pl.BlockSpec(memory_space=pltpu.SMEM),
                pl.BlockSpec((blk, ATT_Q), lambda b, j: (b * nb + j, P_OFF['att_q'] // ATT_Q)),
                kv(k_col, -1), kv(k_col, 0), kv(k_col, 1), kv(v_col, -1), kv(v_col, 0), kv(v_col, 1),
                tab(-1), tab(-1), tab(0), tab(0), tab(1), tab(1),
                pl.BlockSpec((n_ctx, LANE), lambda b, j: (b * ctx_rows, k_col)),
                pl.BlockSpec((n_ctx, LANE), lambda b, j: (b * ctx_rows, v_col)),
                pl.BlockSpec((1, LANE), lambda b, j: (0, 0)),
                pl.BlockSpec((1, LANE), lambda b, j: (0, 0))]
    tile2 = lambda g: jnp.tile(g.reshape(1, ATT_HD), (1, LANE // ATT_HD))
    return pl.pallas_call(
        functools.partial(_att_kernel, nb_ctx=nb_ctx, nb_lat=nb_lat, n_ctx=n_ctx),
        grid=(bsz, nb),
        in_specs=in_specs,
        out_specs=pl.BlockSpec((blk, ATT_Q), lambda b, j: (b * nb + j, 0)),
        out_shape=jax.ShapeDtypeStruct((bsz * nt_rows, ATT_Q), BF16),
        compiler_params=_cparams(2),
        name="attention",
    )(sink.reshape(1, ATT_Q_HEADS), p, p, p, p, p, p, p, cos, sin, cos, sin, cos, sin, p, p, tile2(qg), tile2(kg))


def _conv_kernel(xp_ref, xc_ref, xn_ref, w_ref, b_ref, o_ref, ext_ref, *, nct, nt):
    j = pl.program_id(1)
    rows = xc_ref.shape[0]
    no_prev = (j == 0) | (j == nct)
    no_next = (j == nct - 1) | (j == nt - 1)
    ext_ref[0:8] = jnp.where(no_prev, 0.0, xp_ref[...])
    ext_ref[8:8 + rows] = xc_ref[...]
    ext_ref[8 + rows:16 + rows] = jnp.where(no_next, 0.0, xn_ref[...])
    pad = SSD_CONV // 2
    acc = b_ref[...] + w_ref[0:1] * ext_ref[pl.ds(8 - pad, rows)]
    for t in range(1, SSD_CONV):
        acc = acc + w_ref[t:t + 1] * ext_ref[pl.ds(8 - pad + t, rows)]
    o_ref[...] = _silu(acc)


def _ssd_conv(p, w, bias, bsz, nt_rows, n_ctx):
    tiles, nct = nt_rows // ROW_TILE, n_ctx // ROW_TILE
    ct = 512
    c0 = P_OFF['ssd_xbc'] // ct
    r8 = ROW_TILE // 8
    last8 = bsz * nt_rows // 8 - 1
    return pl.pallas_call(
        functools.partial(_conv_kernel, nct=nct, nt=tiles),
        grid=(bsz, tiles, SSD_XBC // ct),
        in_specs=[pl.BlockSpec((8, ct), lambda b, j, c: (jnp.maximum((b * tiles + j) * r8 - 1, 0), c0 + c)),
                  pl.BlockSpec((ROW_TILE, ct), lambda b, j, c: (b * tiles + j, c0 + c)),
                  pl.BlockSpec((8, ct), lambda b, j, c: (jnp.minimum((b * tiles + j + 1) * r8, last8), c0 + c)),
                  pl.BlockSpec((SSD_CONV, ct), lambda b, j, c: (0, c)),
                  pl.BlockSpec((1, ct), lambda b, j, c: (0, c))],
        out_specs=pl.BlockSpec((ROW_TILE, ct), lambda b, j, c: (b * tiles + j, c)),
        out_shape=jax.ShapeDtypeStruct((bsz * nt_rows, SSD_XBC), F32),
        scratch_shapes=[pltpu.VMEM((ROW_TILE + 16, ct), F32)],
        compiler_params=_cparams(3),
        name="ssd_conv",
    )(p, p, p, w, bias.reshape(1, -1))


def _ssd_kernel(x_ref, bc_ref, sm_ref, dtb_ref, dtbc_ref, al_ref, alc_ref, *rest, rev, final):
    if final:
        yf_ref, z_ref, d_ref, gn_ref, o_ref, st_ref = rest
    else:
        o_ref, st_ref = rest
    L, P, N = SSD_CHUNK, SSD_HD, SSD_STATE
    gw = SSD_HPG * P

    @pl.when(pl.program_id(1) == 0)
    def _():
        st_ref[...] = jnp.zeros_like(st_ref)

    xs = x_ref[...]
    bm, cm = bc_ref[:, 0:SSD_BC], bc_ref[:, SSD_BC:2 * SSD_BC]
    off = SM_DTB if rev else SM_DTF
    sm = sm_ref[...]
    dt = _softplus(sm + dtb_ref[0])
    a = -dt * jnp.exp(al_ref[0])
    dt_t = _softplus(sm.T + dtbc_ref[0])
    a_t = -dt_t * jnp.exp(alc_ref[0])
    tt, ss = _iota((L, L), 0), _iota((L, L), 1)
    causal = (ss >= tt) if rev else (ss <= tt)
    tri = jnp.where(causal, 1.0, 0.0)
    cum = _dot_exact_rhs(tri, a, 3)
    cum_t = _dot_exact_lhs(a_t, tri, 3, NT)
    last = 0 if rev else L - 1
    cum_last = cum[last:last + 1]
    expand = jnp.where(_iota((LANE, SSD_INNER), 0) - off == _iota((LANE, SSD_INNER), 1) // P, 1.0, 0.0)
    ecum = _dot_exact_lhs(jnp.exp(cum), expand, 2)
    wgt = _dot_exact_lhs(jnp.exp(cum_last - cum) * dt, expand, 2)

    st = st_ref[...]
    lane = _iota((1, LANE), 1)
    ys = []
    for g in range(SSD_GROUPS):
        cg, bg = cm[:, N * g:N * (g + 1)], bm[:, N * g:N * (g + 1)]
        cb = _dot(cg, bg, NT)
        for jp in range(SSD_HPG // 2):
            h0 = SSD_HPG * g + 2 * jp
            ms = []
            for h in (off + h0, off + h0 + 1):
                dec = jnp.exp(jnp.where(causal, cum[:, h:h + 1] - cum_t[h:h + 1, :], NEG))
                ms.append(cb * dec * dt_t[h:h + 1, :])
            xp = xs[:, P * h0:P * h0 + LANE]
            rhs = jnp.concatenate([jnp.where(lane < P, xp, 0.0), jnp.where(lane >= P, xp, 0.0)], axis=0)
            ys.append(_dot(jnp.concatenate(ms, axis=1), rhs))
    y = jnp.concatenate(ys, axis=1)
    y = y + jnp.concatenate([_dot(cm[:, N * g:N * (g + 1)], st[:, gw * g:gw * (g + 1)])
                             for g in range(SSD_GROUPS)], axis=1) * ecum
    xw = xs * wgt
    new = jnp.concatenate([_dot(bm[:, N * g:N * (g + 1)], xw[:, gw * g:gw * (g + 1)], TN)
                           for g in range(SSD_GROUPS)], axis=1)
    st_ref[...] = ecum[last:last + 1] * st + new

    if final:
        tot = (y + yf_ref[...] + d_ref[...] * xs) * _silu(z_ref[...])
        outs = []
        for g in range(SSD_GROUPS):
            tg = tot[:, gw * g:gw * (g + 1)]
            outs.append(tg * lax.rsqrt(jnp.mean(tg * tg, axis=-1, keepdims=True) + EPS))
        o_ref[...] = (jnp.concatenate(outs, axis=1) * gn_ref[...]).astype(o_ref.dtype)
    else:
        o_ref[...] = y


def _ssd_pass(p, xbc, dt_bias, a_log, bsz, nt_rows, n_ctx, rev, extra=None):
    L = SSD_CHUNK
    nc, nc_ctx = nt_rows // L, n_ctx // L

    def rows(width, col):
        return pl.BlockSpec((L, width), lambda b, j: (b * nc + _chunk_order(j, nc_ctx, nc, rev), col))

    d = 1 if rev else 0
    vec = pl.BlockSpec((1, 1, LANE), lambda b, j: (d, 0, 0))
    colv = pl.BlockSpec((1, LANE, 1), lambda b, j: (d, 0, 0))
    in_specs = [rows(SSD_INNER, 0), rows(2 * SSD_BC, SSD_INNER // (2 * SSD_BC)), rows(LANE, SMALL_OFF // LANE),
                vec, colv, vec, colv]

    def lanes(v):
        out = jnp.zeros((2, LANE), F32)
        return out.at[0, SM_DTF:SM_DTF + SSD_HEADS].set(v[0]).at[1, SM_DTB:SM_DTB + SSD_HEADS].set(v[1])

    dtb, alog = lanes(dt_bias), lanes(a_log)
    args = [xbc, xbc, p, dtb.reshape(2, 1, LANE), dtb.reshape(2, LANE, 1),
            alog.reshape(2, 1, LANE), alog.reshape(2, LANE, 1)]
    final = extra is not None
    if final:
        y_first, d_skip, gn = extra
        full = pl.BlockSpec((1, SSD_INNER), lambda b, j: (0, 0))
        in_specs += [rows(SSD_INNER, 0), rows(SSD_INNER, P_OFF['ssd_z'] // SSD_INNER), full, full]
        args += [y_first, p, jnp.repeat(d_skip, SSD_HD).reshape(1, SSD_INNER), gn.reshape(1, SSD_INNER)]
    return pl.pallas_call(
        functools.partial(_ssd_kernel, rev=rev, final=final),
        grid=(bsz, nc),
        in_specs=in_specs,
        out_specs=rows(SSD_INNER, 0),
        out_shape=jax.ShapeDtypeStruct((bsz * nt_rows, SSD_INNER), BF16 if final else F32),
        scratch_shapes=[pltpu.VMEM((SSD_STATE, SSD_INNER), F32)],
        compiler_params=_cparams(2),
        name="ssd_bwd_out" if final else "ssd_fwd",
    )(*args)


def _merge_kernel(yg_ref, ya_ref, ys_ref, mg_ref, x_ref, gb_ref, g1_ref, sh2_ref, sc2_ref, n2_ref,
                  wa_ref, wb_ref, wc_ref, wo_ref, xo_ref, h2_ref):
    d = D_MODEL
    gates = _sigmoid(mg_ref[...] + gb_ref[...])
    merged = (gates[:, 0:d] * _dot(yg_ref[...], wa_ref[...])
              + gates[:, d:2 * d] * _dot(ya_ref[...], wb_ref[...])
              + gates[:, 2 * d:3 * d] * _dot(ys_ref[...], wc_ref[...]))
    xn = x_ref[...] + g1_ref[0] * _dot(merged, wo_ref[...])
    xo_ref[...] = xn
    h2_ref[...] = _normmod(xn, n2_ref[...], sh2_ref[0], sc2_ref[0])


def _merge(y_gla, y_att, y_ssd, p, xc, gate_b, mod, norm2_g, wa, wb, wc, wo, bsz, nt_rows, nct):
    tiles = nt_rows // ROW_TILE
    d = D_MODEL

    def rows(width, col=0):
        return pl.BlockSpec((ROW_TILE, width), lambda b, j: (b * tiles + j, col))

    def whole(a):
        return pl.BlockSpec(a.shape, lambda b, j: (0,) * a.ndim)

    gb = gate_b.reshape(1, N_BRANCH * d)
    n2 = norm2_g.reshape(1, d)
    return pl.pallas_call(
        _merge_kernel,
        grid=(bsz, tiles),
        in_specs=[rows(GLA_V), rows(ATT_Q), rows(SSD_INNER), rows(N_BRANCH * d, P_OFF['merge'] // (N_BRANCH * d)),
                  rows(d), whole(gb), _mod_spec(2, nct), _mod_spec(3, nct), _mod_spec(4, nct), whole(n2),
                  whole(wa), whole(wb), whole(wc), whole(wo)],
        out_specs=[rows(d), rows(d)],
        out_shape=[jax.ShapeDtypeStruct(xc.shape, F32), jax.ShapeDtypeStruct(xc.shape, F32)],
        compiler_params=_cparams(2),
        name="merge",
    )(y_gla, y_att, y_ssd, p, xc, gb, mod, mod, mod, n2, wa, wb, wc, wo)


def _router_kernel(h_ref, rw_ref, rb_ref, e_ref, g_ref, r_ref, c_ref, run_ref):
    rows = h_ref.shape[0]

    @pl.when(pl.program_id(0) == 0)
    def _():
        run_ref[...] = jnp.zeros_like(run_ref)

    scores = _sigmoid(_dot_hi(rw_ref[...], h_ref[...], NT))
    biased = scores + rb_ref[...]
    row = lambda a, e: a[e:e + 1, :]
    epg = EXPERTS_PER_GROUP
    best = gidx = None
    for g in range(N_EXPERT_GROUPS):
        vals = [row(biased, epg * g + i) for i in range(epg)]
        top2 = None
        for i in range(epg):
            for i2 in range(i + 1, epg):
                pair = vals[i] + vals[i2]
                top2 = pair if top2 is None else jnp.maximum(top2, pair)
        if g == 0:
            best, gidx = top2, jnp.zeros(top2.shape, jnp.int32)
        else:
            take = top2 > best
            best, gidx = jnp.where(take, top2, best), jnp.where(take, g, gidx)

    def pick(a, i):
        out = row(a, i)
        for g in range(1, N_EXPERT_GROUPS):
            out = jnp.where(gidx == g, row(a, epg * g + i), out)
        return out

    cand = [pick(biased, i) for i in range(epg)]
    raw = [pick(scores, i) for i in range(epg)]

    def argmax_first(vals, exclude):
        bv = bi = bs = None
        for i in range(epg):
            v = vals[i] if exclude is None else jnp.where(exclude == i, -jnp.inf, vals[i])
            if bv is None:
                bv, bi, bs = v, jnp.zeros(v.shape, jnp.int32), raw[0]
            else:
                take = v > bv
                bv, bi, bs = jnp.where(take, v, bv), jnp.where(take, i, bi), jnp.where(take, raw[i], bs)
        return bi, bs

    i0, s0 = argmax_first(cand, None)
    i1, s1 = argmax_first(cand, i0)
    e0, e1 = gidx * epg + i0, gidx * epg + i1
    tot = s0 + s1
    e_ref[...] = jnp.concatenate([e0, e1], axis=0)
    g_ref[...] = jnp.concatenate([s0 / tot, s1 / tot], axis=0)

    eid = _iota((N_EXPERTS, rows), 0)
    oh0 = jnp.where(eid == e0, 1.0, 0.0)
    oh1 = jnp.where(eid == e1, 1.0, 0.0)
    cnt = oh0 + oh1
    before = jnp.where(_iota((rows, rows), 0) < _iota((rows, rows), 1), 1.0, 0.0)
    prefix = _dot(cnt, before) + run_ref[:, 0:1]
    r0 = jnp.sum(oh0 * prefix, axis=0, keepdims=True)
    r1 = jnp.sum(oh1 * prefix, axis=0, keepdims=True)
    r_ref[...] = jnp.concatenate([r0, r1], axis=0).astype(jnp.int32)
    run_ref[...] = run_ref[...] + jnp.sum(cnt, axis=1, keepdims=True)
    c_ref[...] = run_ref[...].astype(jnp.int32)


def _router(h2, router_w, router_b):
    t = h2.shape[0]
    tiles = t // ROW_TILE
    lanes = pl.BlockSpec((2, ROW_TILE), lambda i: (0, i))
    return pl.pallas_call(
        _router_kernel,
        grid=(tiles,),
        in_specs=[pl.BlockSpec((ROW_TILE, D_MODEL), lambda i: (i, 0)),
                  pl.BlockSpec((N_EXPERTS, D_MODEL), lambda i: (0, 0)),
                  pl.BlockSpec((N_EXPERTS, 1), lambda i: (0, 0))],
        out_specs=[lanes, lanes, lanes, pl.BlockSpec((N_EXPERTS, LANE), lambda i: (0, 0))],
        out_shape=[jax.ShapeDtypeStruct((2, t), jnp.int32), jax.ShapeDtypeStruct((2, t), F32),
                   jax.ShapeDtypeStruct((2, t), jnp.int32), jax.ShapeDtypeStruct((N_EXPERTS, LANE), jnp.int32)],
        scratch_shapes=[pltpu.VMEM((N_EXPERTS, LANE), F32)],
        compiler_params=_cparams(1),
        name="router",
    )(h2, router_w.T, router_b.reshape(N_EXPERTS, 1))


def _dispatch_kernel(dest_ref, h_ref, buf_in_ref, buf_ref, sem):
    del buf_in_ref
    rows = h_ref.shape[0]

    def row_copy(r, k):
        return pltpu.make_async_copy(h_ref.at[pl.ds(r, 1)], buf_ref.at[pl.ds(dest_ref[k, r], 1)], sem)

    def start(r, carry):
        row_copy(r, 0).start()
        row_copy(r, 1).start()
        return carry

    lax.fori_loop(0, rows, start, 0, unroll=DMA_UNROLL)
    for _ in range(2):
        pltpu.make_async_copy(h_ref, buf_ref.at[pl.ds(0, rows)], sem).wait()


def _dispatch(h2, dest, n_slots):
    t, d = h2.shape
    return pl.pallas_call(
        _dispatch_kernel,
        grid=(t // ROW_TILE,),
        in_specs=[pl.BlockSpec((2, ROW_TILE), lambda i: (0, i), memory_space=pltpu.SMEM),
                  pl.BlockSpec((ROW_TILE, d), lambda i: (i, 0)),
                  pl.BlockSpec(memory_space=pl.ANY)],
        out_specs=pl.BlockSpec(memory_space=pl.ANY),
        out_shape=jax.ShapeDtypeStruct((n_slots, d), F32),
        scratch_shapes=[pltpu.SemaphoreType.DMA(())],
        input_output_aliases={2: 0},
        compiler_params=_cparams(1),
        name="moe_dispatch",
    )(dest, h2, jnp.zeros((n_slots, d), F32))


def _expert_kernel(be_ref, nu_ref, x_ref, w1_ref, w3_ref, w2_ref, o_ref):
    del be_ref
    used = pl.program_id(0) < nu_ref[0]

    @pl.when(used)
    def _():
        x = x_ref[...].astype(BF16)
        hid = _silu(_dot(x, w1_ref[0])) * _dot(x, w3_ref[0])
        o_ref[...] = _dot(hid, w2_ref[0])

    @pl.when(jnp.logical_not(used))
    def _():
        o_ref[...] = jnp.zeros_like(o_ref)


def _experts(buf, block_e, n_used, w1, w3, w2):
    n_slots, d = buf.shape
    ff = w1.shape[-1]
    grid_spec = pltpu.PrefetchScalarGridSpec(
        num_scalar_prefetch=2,
        grid=(n_slots // MOE_BLOCK,),
        in_specs=[pl.BlockSpec((MOE_BLOCK, d), lambda i, be, nu: (i, 0)),
                  pl.BlockSpec((1, d, ff), lambda i, be, nu: (be[i], 0, 0)),
                  pl.BlockSpec((1, d, ff), lambda i, be, nu: (be[i], 0, 0)),
                  pl.BlockSpec((1, ff, d), lambda i, be, nu: (be[i], 0, 0))],
        out_specs=pl.BlockSpec((MOE_BLOCK, d), lambda i, be, nu: (i, 0)),
    )
    return pl.pallas_call(
        _expert_kernel,
        grid_spec=grid_spec,
        out_shape=jax.ShapeDtypeStruct((n_slots, d), F32),
        compiler_params=_cparams(1),
        name="moe_experts",
    )(block_e, n_used, buf, w1, w3, w2)


def _combine_kernel(dest_ref, g_ref, x_ref, g2_ref, buf_ref, o_ref, rows_ref, sem):
    rows = x_ref.shape[0]

    def row_copy(r, k):
        return pltpu.make_async_copy(buf_ref.at[pl.ds(dest_ref[k, r], 1)], rows_ref.at[k, pl.ds(r, 1)], sem)

    def start(r, carry):
        row_copy(r, 0).start()
        row_copy(r, 1).start()
        return carry

    lax.fori_loop(0, rows, start, 0, unroll=DMA_UNROLL)
    for k in range(2):
        pltpu.make_async_copy(buf_ref.at[pl.ds(0, rows)], rows_ref.at[k], sem).wait()
    y = g_ref[:, 0:1] * rows_ref[0] + g_ref[:, 1:2] * rows_ref[1]
    o_ref[...] = x_ref[...] + g2_ref[0] * y


def _combine(out_buf, dest, gates_col, x_new, mod, bsz, nt_rows, nct, latent_only):
    tiles = nt_rows // ROW_TILE
    d = D_MODEL
    skip = nct if latent_only else 0
    out_tiles = tiles - skip
    src = lambda b, j: b * tiles + skip + j
    return pl.pallas_call(
        _combine_kernel,
        grid=(bsz, out_tiles),
        in_specs=[pl.BlockSpec((2, ROW_TILE), lambda b, j: (0, src(b, j)), memory_space=pltpu.SMEM),
                  pl.BlockSpec((ROW_TILE, 2), lambda b, j: (src(b, j), 0)),
                  pl.BlockSpec((ROW_TILE, d), lambda b, j: (src(b, j), 0)),
                  pl.BlockSpec((1, 1, d), lambda b, j: (2 * b + jnp.where(skip + j >= nct, 1, 0), 0, 5)),
                  pl.BlockSpec(memory_space=pl.ANY)],
        out_specs=pl.BlockSpec((ROW_TILE, d), lambda b, j: (b * out_tiles + j, 0)),
        out_shape=jax.ShapeDtypeStruct((bsz * out_tiles * ROW_TILE, d), F32),
        scratch_shapes=[pltpu.VMEM((2, ROW_TILE, d), F32), pltpu.SemaphoreType.DMA(())],
        compiler_params=_cparams(2),
        name="moe_combine",
    )(dest, gates_col, x_new, mod, out_buf)


def _moe(h2, x_new, mod, router_w, router_b, w1, w3, w2, bsz, nt_rows, nct, latent_only):
    t = h2.shape[0]
    e_idx, gates, rank, counts = _router(h2, router_w, router_b)
    counts = counts[:, 0]
    padded = (counts + MOE_BLOCK - 1) // MOE_BLOCK * MOE_BLOCK
    pad_end = jnp.cumsum(padded)
    pad_start = pad_end - padded
    first = jnp.sum(jnp.where(e_idx[None] == jnp.arange(N_EXPERTS)[:, None, None], pad_start[:, None, None], 0),
                    axis=0)
    dest = first + rank
    n_blocks = -(-2 * t // MOE_BLOCK) + N_EXPERTS
    block_e = jnp.minimum(jnp.searchsorted(pad_end, jnp.arange(n_blocks, dtype=jnp.int32) * MOE_BLOCK,
                                           side='right'), N_EXPERTS - 1).astype(jnp.int32)
    n_used = (pad_end[-1:] // MOE_BLOCK).astype(jnp.int32)
    buf = _dispatch(h2, dest, n_blocks * MOE_BLOCK)
    out_buf = _experts(buf, block_e, n_used, w1, w3, w2)
    return _combine(out_buf, dest, gates.T, x_new, mod, bsz, nt_rows, nct, latent_only)


def _permute_w_in(w_in):
    offsets = np.concatenate([[0], np.cumsum(IN_SPLITS)])
    start = dict(zip(IN_NAMES, offsets[:-1]))
    width = dict(zip(IN_NAMES, IN_SPLITS))
    cols = [w_in[:, start[n]:start[n] + width[n]] for n in P_ORDER]
    cols.append(jnp.zeros((w_in.shape[0], SMALL_PAD), w_in.dtype))
    return jnp.concatenate(cols, axis=1).astype(BF16)


def _layer(xc, mod, lp, router_w, router_b, rope, bsz, nt_rows, n_ctx, last):
    nct = n_ctx // ROW_TILE
    h = _norm_modulate(xc, lp['norm1_g'], mod, bsz, nt_rows, nct)
    p = _in_proj(h, _permute_w_in(lp['w_in']))

    o_f = _gla_pass(p, lp['gla_w2'], lp['gla_b2'], bsz, nt_rows, n_ctx, rev=False)
    y_gla = _gla_pass(p, lp['gla_w2'], lp['gla_b2'], bsz, nt_rows, n_ctx, rev=True,
                      extra=(o_f, lp['gla_norm_g']))
    y_att = _attention(p, rope[0], rope[1], lp['att_sink'], lp['q_norm_g'], lp['k_norm_g'], bsz, nt_rows, n_ctx)
    xbc = _ssd_conv(p, lp['ssd_conv_w'], lp['ssd_conv_b'], bsz, nt_rows, n_ctx)
    y_f = _ssd_pass(p, xbc, lp['ssd_dt_bias'], lp['ssd_a_log'], bsz, nt_rows, n_ctx, rev=False)
    y_ssd = _ssd_pass(p, xbc, lp['ssd_dt_bias'], lp['ssd_a_log'], bsz, nt_rows, n_ctx, rev=True,
                      extra=(y_f, lp['ssd_d'], lp['ssd_norm_g']))

    x_new, h2 = _merge(y_gla, y_att, y_ssd, p, xc, lp['gate_b'], mod, lp['norm2_g'],
                       lp['w_br_a'].astype(BF16), lp['w_br_b'].astype(BF16), lp['w_br_c'].astype(BF16),
                       lp['w_out'].astype(BF16), bsz, nt_rows, nct)
    return _moe(h2, x_new, mod, router_w, router_b, lp['w1'].astype(BF16), lp['w3'].astype(BF16),
                lp['w2'].astype(BF16), bsz, nt_rows, nct, latent_only=last)


def kernel(x, c, ctx, c_ctx, w_mod, b_mod, norm1_g, w_in, gla_w2, gla_b2, gla_norm_g, q_norm_g, k_norm_g, att_sink, ssd_conv_w, ssd_conv_b, ssd_dt_bias, ssd_a_log, ssd_d, ssd_norm_g, gate_b, w_br_a, w_br_b, w_br_c, w_out, norm2_g, router_w, router_b, w1, w3, w2):
    bsz, seq, d = x.shape
    n_ctx = ctx.shape[1]
    nt_rows = n_ctx + seq
    assert d == D_MODEL and n_ctx % ROW_TILE == 0 and seq % ROW_TILE == 0 and nt_rows % n_ctx == 0
    assert seq % GRID_W == 0 and bsz < 16

    cc = jnp.zeros((16, d), F32).at[:bsz].set(c).at[bsz].set(c_ctx)
    mods = _mod_vectors(cc, w_mod, b_mod)
    rope = _rope_tables(seq)
    xc = jnp.concatenate([ctx, x], axis=1).reshape(bsz * nt_rows, d)
    params = dict(norm1_g=norm1_g, w_in=w_in, gla_w2=gla_w2, gla_b2=gla_b2, gla_norm_g=gla_norm_g,
                  q_norm_g=q_norm_g, k_norm_g=k_norm_g, att_sink=att_sink, ssd_conv_w=ssd_conv_w,
                  ssd_conv_b=ssd_conv_b, ssd_dt_bias=ssd_dt_bias, ssd_a_log=ssd_a_log, ssd_d=ssd_d,
                  ssd_norm_g=ssd_norm_g, gate_b=gate_b, w_br_a=w_br_a, w_br_b=w_br_b, w_br_c=w_br_c,
                  w_out=w_out, norm2_g=norm2_g, w1=w1, w3=w3, w2=w2)
    for l in range(DEPTH):
        lp = {name: val[l] for name, val in params.items()}
        m = mods[l]
        mod = jnp.stack([jnp.broadcast_to(m[bsz], (bsz, N_MOD * d)), m[:bsz]], axis=1)
        mod = mod.reshape(2 * bsz, 1, N_MOD * d)
        xc = _layer(xc, mod, lp, router_w, router_b, rope, bsz, nt_rows, n_ctx, last=(l == DEPTH - 1))
    return xc.reshape(bsz, seq, d)
```

```python
import functools

import numpy as np
import jax
import jax.numpy as jnp
from jax import lax
from jax.experimental import pallas as pl
from jax.experimental.pallas import tpu as pltpu

F32 = jnp.float32
BF16 = jnp.bfloat16

D_MODEL = 1024
DEPTH = 2
GRID_W = 64
EPS = 1e-6
N_MOD = 6

GLA_HEADS = 4
GLA_DK = 64
GLA_DV = 128
GLA_GATE_RANK = 16
GLA_GATE_TEMP = 16.0
GLA_CHUNK = 64
GLA_SUB = 8
GLA_STEP = 4
GLA_QK = GLA_HEADS * GLA_DK
GLA_V = GLA_HEADS * GLA_DV

ATT_Q_HEADS = 8
ATT_KV_HEADS = 2
ATT_HD = 64
ATT_WINDOW = 128
ATT_BLOCK = 128
ROPE_BASE = 10000.0
ATT_Q = ATT_Q_HEADS * ATT_HD
ATT_KV = ATT_KV_HEADS * ATT_HD

SSD_HEADS = 16
SSD_HD = 64
SSD_GROUPS = 2
SSD_STATE = 128
SSD_CONV = 5
SSD_CHUNK = 128
SSD_INNER = SSD_HEADS * SSD_HD
SSD_BC = SSD_GROUPS * SSD_STATE
SSD_XBC = SSD_INNER + 2 * SSD_BC
SSD_HPG = SSD_HEADS // SSD_GROUPS

N_BRANCH = 3
N_EXPERTS = 16
N_EXPERT_GROUPS = 4
EXPERTS_PER_GROUP = N_EXPERTS // N_EXPERT_GROUPS
EXPERT_FF = 1024

IN_NAMES = ('gla_q', 'gla_k', 'gla_v', 'gla_r', 'gla_gf', 'gla_gb', 'att_q', 'att_k', 'att_v',
            'ssd_z', 'ssd_xbc', 'ssd_dtf', 'ssd_dtb', 'merge')
IN_SPLITS = (GLA_QK, GLA_QK, GLA_V, GLA_V, GLA_GATE_RANK, GLA_GATE_RANK, ATT_Q, ATT_KV, ATT_KV,
             SSD_INNER, SSD_XBC, SSD_HEADS, SSD_HEADS, N_BRANCH * D_MODEL)

P_ORDER = ('merge', 'ssd_z', 'ssd_xbc', 'gla_v', 'gla_r', 'att_q', 'gla_q', 'gla_k', 'att_k', 'att_v',
           'gla_gf', 'gla_gb', 'ssd_dtf', 'ssd_dtb')
LANE = 128
ID_LANES = LANE
SMALL_PAD = 4 * 16
N_P = sum(IN_SPLITS) + SMALL_PAD
H_EXT = D_MODEL + ID_LANES


def _p_offsets():
    width = dict(zip(IN_NAMES, IN_SPLITS))
    off, o = {}, 0
    for name in P_ORDER:
        off[name] = o
        o += width[name]
    return off


P_OFF = _p_offsets()
SMALL_OFF = P_OFF['gla_gf']
SM_GF, SM_GB, SM_DTF, SM_DTB = 0, 16, 32, 48

ROW_TILE = 256
MOE_BLOCK = 256
NEG = -1e30
DMA_UNROLL = 8
VMEM_LIMIT = 56 * 1024 * 1024

NN = (((1,), (0,)), ((), ()))
NT = (((1,), (1,)), ((), ()))
TN = (((0,), (0,)), ((), ()))


def _dot(a, b, dims=NN):
    return lax.dot_general(a.astype(BF16), b.astype(BF16), dims, preferred_element_type=F32)


def _dot_hi(a, b, dims=NN):
    return lax.dot_general(a.astype(F32), b.astype(F32), dims, precision=lax.Precision.HIGHEST,
                           preferred_element_type=F32)


def _split(x, terms):
    parts, rem = [], x
    for i in range(terms):
        part = rem.astype(BF16)
        parts.append(part)
        if i + 1 < terms:
            rem = rem - part.astype(F32)
    return parts


def _dot_exact_rhs(m01, x, terms, dims=NN):
    m01 = m01.astype(BF16)
    return sum(lax.dot_general(m01, part, dims, preferred_element_type=F32) for part in _split(x, terms))


def _dot_exact_lhs(x, m01, terms, dims=NN):
    m01 = m01.astype(BF16)
    return sum(lax.dot_general(part, m01, dims, preferred_element_type=F32) for part in _split(x, terms))


def _dot3(a, b, dims=NN):
    (ah, al), (bh, bl) = _split(a, 2), _split(b, 2)
    mm = lambda x, y: lax.dot_general(x, y, dims, preferred_element_type=F32)
    return mm(ah, bh) + mm(ah, bl) + mm(al, bh)


def _sigmoid(x):
    return 1.0 / (1.0 + jnp.exp(-x))


def _silu(x):
    return x * _sigmoid(x)


def _softplus(x):
    return jnp.maximum(x, 0.0) + jnp.log(1.0 + jnp.exp(-jnp.abs(x)))


def _iota(shape, dim):
    return lax.broadcasted_iota(jnp.int32, shape, dim)


def _cparams(n_axes):
    return pltpu.CompilerParams(dimension_semantics=("arbitrary",) * n_axes, vmem_limit_bytes=VMEM_LIMIT)


def _largest_tile(n, cap, mult):
    t = (min(cap, n) // mult) * mult
    while n % t:
        t -= mult
    return t


def _mod_kernel(c_ref, w_ref, b_ref, o_ref):
    o_ref[0] = _dot_hi(_silu(c_ref[...]), w_ref[0]) + b_ref[0]


def _mod_vectors(cc, w_mod, b_mod):
    n_l, d, n6 = w_mod.shape
    tn = 1024
    return pl.pallas_call(
        _mod_kernel,
        grid=(n_l, n6 // tn),
        in_specs=[pl.BlockSpec(cc.shape, lambda l, j: (0, 0)),
                  pl.BlockSpec((1, d, tn), lambda l, j: (l, 0, j)),
                  pl.BlockSpec((1, 1, tn), lambda l, j: (l, 0, j))],
        out_specs=pl.BlockSpec((1, cc.shape[0], tn), lambda l, j: (l, 0, j)),
        out_shape=jax.ShapeDtypeStruct((n_l, cc.shape[0], n6), F32),
        compiler_params=_cparams(2),
        name="mod_vectors",
    )(cc, w_mod, b_mod.reshape(n_l, 1, n6))


def _mod_spec(which, nct):
    return pl.BlockSpec((1, 1, D_MODEL), lambda b, j: (2 * b + jnp.where(j >= nct, 1, 0), 0, which))


def _normmod(x, g, shift, scale):
    y = x * lax.rsqrt(jnp.mean(x * x, axis=-1, keepdims=True) + EPS) * g
    return y * (1.0 + scale) + shift


def _normmod_kernel(x_ref, g_ref, sh_ref, sc_ref, o_ref):
    o_ref[...] = _normmod(x_ref[...], g_ref[...], sh_ref[0], sc_ref[0]).astype(o_ref.dtype)


def _norm_modulate(xc, g, mod, bsz, nt_rows, nct):
    tiles = nt_rows // ROW_TILE
    return pl.pallas_call(
        _normmod_kernel,
        grid=(bsz, tiles),
        in_specs=[pl.BlockSpec((ROW_TILE, D_MODEL), lambda b, j: (b * tiles + j, 0)),
                  pl.BlockSpec((1, D_MODEL), lambda b, j: (0, 0)),
                  _mod_spec(0, nct), _mod_spec(1, nct)],
        out_specs=pl.BlockSpec((ROW_TILE, D_MODEL), lambda b, j: (b * tiles + j, 0)),
        out_shape=jax.ShapeDtypeStruct(xc.shape, BF16),
        compiler_params=_cparams(2),
        name="norm_modulate",
    )(xc, g.reshape(1, -1), mod, mod)


def _matmul_kernel(a_ref, w_ref, o_ref):
    o_ref[...] = jnp.dot(a_ref[...], w_ref[...], preferred_element_type=F32).astype(o_ref.dtype)


def _in_proj(h, w):
    m, kdim = h.shape
    n = w.shape[1]
    tm = _largest_tile(m, 2048, 256)
    tn = 1152
    return pl.pallas_call(
        _matmul_kernel,
        grid=(m // tm, n // tn),
        in_specs=[pl.BlockSpec((tm, kdim), lambda i, j: (i, 0)),
                  pl.BlockSpec((kdim, tn), lambda i, j: (0, j))],
        out_specs=pl.BlockSpec((tm, tn), lambda i, j: (i, j)),
        out_shape=jax.ShapeDtypeStruct((m, n), F32),
        compiler_params=_cparams(2),
        name="in_proj",
    )(h, w)


def _chunk_order(j, nc_ctx, nc, rev):
    if not rev:
        return j
    return jnp.where(j < nc_ctx, nc_ctx - 1 - j, nc - 1 - (j - nc_ctx))


def _gla_kernel(q_ref, k_ref, v_ref, sm_ref, w2_ref, b2_ref, *rest, rev, final):
    if final:
        of_ref, r_ref, gn_ref, o_ref, st_ref, b_ref = rest
    else:
        o_ref, st_ref, b_ref = rest
    L, S = GLA_CHUNK, GLA_SUB
    rows = q_ref.shape[0]

    @pl.when(pl.program_id(1) == 0)
    def _():
        st_ref[...] = jnp.zeros_like(st_ref)

    logits = _dot3(sm_ref[...], w2_ref[0]) + b2_ref[0]
    g = (jnp.minimum(logits, 0.0) - jnp.log(1.0 + jnp.exp(-jnp.abs(logits)))) * (1.0 / GLA_GATE_TEMP)
    rr, cc = _iota((rows, rows), 0), _iota((rows, rows), 1)
    tri = jnp.where((rr // L == cc // L) & ((cc >= rr) if rev else (cc <= rr)), 1.0, 0.0)
    b_ref[...] = _dot_exact_rhs(tri, g, 3)

    head_of_lane = _iota((1, GLA_QK), 1) // GLA_DK
    seg = jnp.where(_iota((GLA_QK, GLA_V), 0) // GLA_DK == _iota((GLA_QK, GLA_V), 1) // GLA_DV, 1.0, 0.0)
    sub_row = _iota((S, 1), 0)
    by_head = lambda x: [jnp.where(head_of_lane == h, x, 0.0) for h in range(GLA_HEADS)]
    n_sub = L // S
    order = list(reversed(range(rows // L))) if rev else list(range(rows // L))
    last = 0 if rev else L - 1

    groups = []
    for i in range(L // (2 * S)):
        r0 = 2 * S * i
        if (not rev) and i > 0:
            groups.append((r0, 2 * S, 0, r0, r0 - 1))
        if rev and r0 + 2 * S < L:
            groups.append((r0, 2 * S, r0 + 2 * S, L - r0 - 2 * S, r0 + 2 * S))
        groups.append((r0, S, r0 + S, S, r0 + S) if rev else (r0 + S, S, r0, S, r0 + S - 1))

    vals = {c: (q_ref[L * c:L * (c + 1)] * (GLA_DK ** -0.5), k_ref[L * c:L * (c + 1)],
                v_ref[L * c:L * (c + 1)], b_ref[L * c:L * (c + 1)]) for c in order}

    scores = {}
    for c in order:
        q, k, v, b = vals[c]
        for gi, (q0, nq, k0, nk, edge) in enumerate(groups):
            ref = b[edge:edge + 1]
            qd = q[q0:q0 + nq] * jnp.exp(b[q0:q0 + nq] - ref)
            kd = k[k0:k0 + nk] * jnp.exp(ref - b[k0:k0 + nk])
            scores[c, gi] = _dot(jnp.concatenate(by_head(qd), axis=0), kd, NT)

    ps = []
    for c in order:
        q, k, v, b = vals[c]
        for i in range(n_sub):
            qi, ki, bi = q[S * i:S * (i + 1)], k[S * i:S * (i + 1)], b[S * i:S * (i + 1)]
            for s in range(S):
                valid = (sub_row <= s) if rev else (sub_row >= s)
                ps.append(qi * jnp.exp(jnp.where(valid, bi - bi[s:s + 1], NEG)) * ki[s:s + 1])
    w_all = _dot(jnp.concatenate(ps, axis=0), seg)

    intra = {}
    for ci, c in enumerate(order):
        q, k, v, b = vals[c]
        pieces = [None] * n_sub

        def add(i, val):
            pieces[i] = val if pieces[i] is None else pieces[i] + val

        for gi, (q0, nq, k0, nk, edge) in enumerate(groups):
            s_g = scores[c, gi]
            val = jnp.concatenate([_dot(s_g[nq * h:nq * (h + 1)], v[k0:k0 + nk, GLA_DV * h:GLA_DV * (h + 1)])
                                   for h in range(GLA_HEADS)], axis=1)
            for j in range(nq // S):
                add(q0 // S + j, val[S * j:S * (j + 1)])
        for i in range(n_sub):
            vi = v[S * i:S * (i + 1)]
            base = (ci * n_sub + i) * S * S
            for s in range(S):
                add(i, w_all[base + S * s:base + S * (s + 1)] * vi[s:s + 1])
        intra[c] = jnp.concatenate(pieces, axis=0)

    st = st_ref[...]
    for c in order:
        q, k, v, b = vals[c]
        b_last = b[last:last + 1]
        o = intra[c] + jnp.concatenate([_dot(qh, st, NT) for qh in by_head(q * jnp.exp(b))], axis=1)
        kk = k * jnp.exp(b_last - b)
        new = None
        for h in range(GLA_HEADS):
            u = _dot(v[:, GLA_DV * h:GLA_DV * (h + 1)], kk, TN)
            new = u if new is None else jnp.where(head_of_lane == h, u, new)
        st = jnp.exp(b_last) * st + new
        if final:
            tot = o + of_ref[L * c:L * (c + 1)]
            outs = []
            for h in range(GLA_HEADS):
                oh = tot[:, GLA_DV * h:GLA_DV * (h + 1)]
                outs.append(oh * lax.rsqrt(jnp.mean(oh * oh, axis=-1, keepdims=True) + EPS) * gn_ref[...])
            o_ref[L * c:L * (c + 1)] = (jnp.concatenate(outs, axis=1) * _silu(r_ref[L * c:L * (c + 1)])).astype(o_ref.dtype)
        else:
            o_ref[L * c:L * (c + 1)] = o
    st_ref[...] = st


def _gla_pass(p, w2, b2, bsz, nt_rows, n_ctx, rev, extra=None):
    rows_per_step = GLA_STEP * GLA_CHUNK
    nb, nb_ctx = nt_rows // rows_per_step, n_ctx // rows_per_step

    def rows(width, col):
        return pl.BlockSpec((rows_per_step, width), lambda b, j: (b * nb + _chunk_order(j, nb_ctx, nb, rev), col))

    d = 1 if rev else 0
    in_specs = [rows(GLA_QK, P_OFF['gla_q'] // GLA_QK), rows(GLA_QK, P_OFF['gla_k'] // GLA_QK),
                rows(GLA_V, P_OFF['gla_v'] // GLA_V), rows(LANE, SMALL_OFF // LANE),
                pl.BlockSpec((1, LANE, GLA_QK), lambda b, j: (d, 0, 0)),
                pl.BlockSpec((1, 1, GLA_QK), lambda b, j: (d, 0, 0))]
    w2_lanes = jnp.zeros((2, LANE, GLA_QK), F32)
    w2_lanes = w2_lanes.at[0, SM_GF:SM_GF + GLA_GATE_RANK].set(w2[0]).at[1, SM_GB:SM_GB + GLA_GATE_RANK].set(w2[1])
    args = [p, p, p, p, w2_lanes, b2.reshape(2, 1, GLA_QK)]
    final = extra is not None
    if final:
        o_first, gn = extra
        in_specs += [rows(GLA_V, 0), rows(GLA_V, P_OFF['gla_r'] // GLA_V),
                     pl.BlockSpec((1, GLA_DV), lambda b, j: (0, 0))]
        args += [o_first, p, gn.reshape(1, GLA_DV)]
    return pl.pallas_call(
        functools.partial(_gla_kernel, rev=rev, final=final),
        grid=(bsz, nb),
        in_specs=in_specs,
        out_specs=rows(GLA_V, 0),
        out_shape=jax.ShapeDtypeStruct((bsz * nt_rows, GLA_V), BF16 if final else F32),
        scratch_shapes=[pltpu.VMEM((GLA_DV, GLA_QK), F32), pltpu.VMEM((rows_per_step, GLA_QK), F32)],
        compiler_params=_cparams(2),
        name="gla_bwd_out" if final else "gla_fwd",
    )(*args)


def _att_kernel(sink_ref, q_ref, kp_ref, kc_ref, kn_ref, vp_ref, vc_ref, vn_ref,
                cp_ref, sp_ref, cc_ref, sc_ref, cn_ref, sn_ref, kx_ref, vx_ref, qg_ref, kg_ref,
                o_ref, *, nb_ctx, nb_lat, n_ctx):
    j = pl.program_id(1)
    blk = ATT_BLOCK
    lane = _iota((1, LANE), 1)
    half_mean = jnp.where(_iota((LANE, LANE), 0) // ATT_HD == _iota((LANE, LANE), 1) // ATT_HD,
                          1.0 / ATT_HD, 0.0)

    def norm(x, g_ref):
        return x * lax.rsqrt(_dot(x * x, half_mean) + EPS) * g_ref[...]

    def rope(x, c_ref, s_ref):
        swapped = jnp.where(lane % 32 < 16, pltpu.roll(x, LANE - 16, 1), pltpu.roll(x, 16, 1))
        return x * c_ref[...] + swapped * s_ref[...]

    def attend(qpairs, keys, vals, bias):
        keys_sw = pltpu.roll(keys, ATT_HD, 1).astype(BF16)
        vals_sw = pltpu.roll(vals, ATT_HD, 1).astype(BF16)
        keys, vals = keys.astype(BF16), vals.astype(BF16)
        rep = ATT_Q_HEADS // ATT_KV_HEADS
        heads = range(ATT_Q_HEADS)
        straight = [(h % 2) == (h // rep) for h in heads]
        scores = []
        for h in heads:
            qm = jnp.where((lane >= ATT_HD) if h % 2 else (lane < ATT_HD), qpairs[h // 2], 0.0)
            s = _dot(qm, keys if straight[h] else keys_sw, NT)
            scores.append(s if bias is None else s + bias)
        probs, dens = [], []
        for h in heads:
            sk = sink_ref[0, h]
            m = jnp.maximum(jnp.max(scores[h], axis=-1, keepdims=True), sk)
            pr = jnp.exp(scores[h] - m)
            probs.append(pr)
            dens.append(jnp.sum(pr, axis=-1, keepdims=True) + jnp.exp(sk - m))
        outs = [_dot(probs[h], vals if straight[h] else vals_sw) / dens[h] for h in heads]
        o_ref[...] = jnp.concatenate([jnp.where(lane < ATT_HD, outs[2 * pi], outs[2 * pi + 1])
                                      for pi in range(ATT_Q_HEADS // 2)], axis=1).astype(o_ref.dtype)

    kx = norm(kx_ref[...], kg_ref)
    vx = vx_ref[...]

    @pl.when(j < nb_ctx)
    def _():
        qpairs = [norm(q_ref[:, LANE * pi:LANE * (pi + 1)], qg_ref) * (ATT_HD ** -0.5)
                  for pi in range(ATT_Q_HEADS // 2)]
        attend(qpairs, kx, vx, None)

    @pl.when(j >= nb_ctx)
    def _():
        li = j - nb_ctx
        qpairs = [rope(norm(q_ref[:, LANE * pi:LANE * (pi + 1)], qg_ref), cc_ref, sc_ref) * (ATT_HD ** -0.5)
                  for pi in range(ATT_Q_HEADS // 2)]
        keys = jnp.concatenate([kx, rope(norm(kp_ref[...], kg_ref), cp_ref, sp_ref),
                                rope(norm(kc_ref[...], kg_ref), cc_ref, sc_ref),
                                rope(norm(kn_ref[...], kg_ref), cn_ref, sn_ref)], axis=0)
        vals = jnp.concatenate([vx, vp_ref[...], vc_ref[...], vn_ref[...]], axis=0)
        a, jj = _iota((blk, 3 * blk), 0), _iota((blk, 3 * blk), 1)
        ok = (jj >= a) & (jj <= a + 2 * ATT_WINDOW)
        ok = ok & ((jj >= blk) | (li > 0)) & ((jj < 2 * blk) | (li < nb_lat - 1))
        bias = jnp.concatenate([jnp.zeros((blk, n_ctx), F32), jnp.where(ok, 0.0, NEG)], axis=1)
        attend(qpairs, keys, vals, bias)


def _rope_tables(n):
    rows = n // GRID_W
    row = jnp.repeat(jnp.arange(rows), GRID_W).astype(F32)
    col = jnp.tile(jnp.arange(GRID_W), rows).astype(F32)
    axis_dim = ATT_HD // 2
    inv_freq = ROPE_BASE ** (-jnp.arange(0, axis_dim, 2, dtype=F32) / axis_dim)
    ang_r, ang_c = row[:, None] * inv_freq, col[:, None] * inv_freq
    cos = jnp.concatenate([jnp.cos(ang_r), jnp.cos(ang_r), jnp.cos(ang_c), jnp.cos(ang_c)], axis=1)
    sin = jnp.concatenate([-jnp.sin(ang_r), jnp.sin(ang_r), -jnp.sin(ang_c), jnp.sin(ang_c)], axis=1)
    return jnp.tile(cos, (1, LANE // ATT_HD)), jnp.tile(sin, (1, LANE // ATT_HD))


def _attention(p, cos, sin, sink, qg, kg, bsz, nt_rows, n_ctx):
    blk = ATT_BLOCK
    nb, nb_ctx = nt_rows // blk, n_ctx // blk
    nb_lat = nb - nb_ctx
    k_col, v_col = P_OFF['att_k'] // LANE, P_OFF['att_v'] // LANE

    def lat(j, d):
        return jnp.clip(j - nb_ctx + d, 0, nb_lat - 1)

    def kv(col, d):
        return pl.BlockSpec((blk, LANE), lambda b, j: (b * nb + nb_ctx + lat(j, d), col))

    def tab(d):
        return pl.BlockSpec((blk, LANE), lambda b, j: (lat(j, d), 0))

    ctx_rows = nt_rows // n_ctx
    in_specs = [pl.BlockSpec(memory_space=pltpu.SMEM),
                pl.BlockSpec((blk, ATT_Q), lambda b, j: (b * nb + j, P_OFF['att_q'] // ATT_Q)),
                kv(k_col, -1), kv(k_col, 0), kv(k_col, 1), kv(v_col, -1), kv(v_col, 0), kv(v_col, 1),
                tab(-1), tab(-1), tab(0), tab(0), tab(1), tab(1),
                pl.BlockSpec((n_ctx, LANE), lambda b, j: (b * ctx_rows, k_col)),
                pl.BlockSpec((n_ctx, LANE), lambda b, j: (b * ctx_rows, v_col)),
                pl.BlockSpec((1, LANE), lambda b, j: (0, 0)),
                pl.BlockSpec((1, LANE), lambda b, j: (0, 0))]
    tile2 = lambda g: jnp.tile(g.reshape(1, ATT_HD), (1, LANE // ATT_HD))
    return pl.pallas_call(
        functools.partial(_att_kernel, nb_ctx=nb_ctx, nb_lat=nb_lat, n_ctx=n_ctx),
        grid=(bsz, nb),
        in_specs=in_specs,
        out_specs=pl.BlockSpec((blk, ATT_Q), lambda b, j: (b * nb + j, 0)),
        out_shape=jax.ShapeDtypeStruct((bsz * nt_rows, ATT_Q), BF16),
        compiler_params=_cparams(2),
        name="attention",
    )(sink.reshape(1, ATT_Q_HEADS), p, p, p, p, p, p, p, cos, sin, cos, sin, cos, sin, p, p, tile2(qg), tile2(kg))


def _conv_kernel(xp_ref, xc_ref, xn_ref, w_ref, b_ref, o_ref, ext_ref, *, nct, nt):
    j = pl.program_id(1)
    rows = xc_ref.shape[0]
    no_prev = (j == 0) | (j == nct)
    no_next = (j == nct - 1) | (j == nt - 1)
    ext_ref[0:8] = jnp.where(no_prev, 0.0, xp_ref[...])
    ext_ref[8:8 + rows] = xc_ref[...]
    ext_ref[8 + rows:16 + rows] = jnp.where(no_next, 0.0, xn_ref[...])
    pad = SSD_CONV // 2
    acc = b_ref[...] + w_ref[0:1] * ext_ref[pl.ds(8 - pad, rows)]
    for t in range(1, SSD_CONV):
        acc = acc + w_ref[t:t + 1] * ext_ref[pl.ds(8 - pad + t, rows)]
    o_ref[...] = _silu(acc)


def _ssd_conv(p, w, bias, bsz, nt_rows, n_ctx):
    tiles, nct = nt_rows // ROW_TILE, n_ctx // ROW_TILE
    ct = 512
    c0 = P_OFF['ssd_xbc'] // ct
    r8 = ROW_TILE // 8
    last8 = bsz * nt_rows // 8 - 1
    return pl.pallas_call(
        functools.partial(_conv_kernel, nct=nct, nt=tiles),
        grid=(bsz, tiles, SSD_XBC // ct),
        in_specs=[pl.BlockSpec((8, ct), lambda b, j, c: (jnp.maximum((b * tiles + j) * r8 - 1, 0), c0 + c)),
                  pl.BlockSpec((ROW_TILE, ct), lambda b, j, c: (b * tiles + j, c0 + c)),
                  pl.BlockSpec((8, ct), lambda b, j, c: (jnp.minimum((b * tiles + j + 1) * r8, last8), c0 + c)),
                  pl.BlockSpec((SSD_CONV, ct), lambda b, j, c: (0, c)),
                  pl.BlockSpec((1, ct), lambda b, j, c: (0, c))],
        out_specs=pl.BlockSpec((ROW_TILE, ct), lambda b, j, c: (b * tiles + j, c)),
        out_shape=jax.ShapeDtypeStruct((bsz * nt_rows, SSD_XBC), F32),
        scratch_shapes=[pltpu.VMEM((ROW_TILE + 16, ct), F32)],
        compiler_params=_cparams(3),
        name="ssd_conv",
    )(p, p, p, w, bias.reshape(1, -1))


def _ssd_kernel(x_ref, bc_ref, sm_ref, dtb_ref, dtbc_ref, al_ref, alc_ref, *rest, rev, final):
    if final:
        yf_ref, z_ref, d_ref, gn_ref, o_ref, st_ref = rest
    else:
        o_ref, st_ref = rest
    L, P, N = SSD_CHUNK, SSD_HD, SSD_STATE
    gw = SSD_HPG * P

    @pl.when(pl.program_id(1) == 0)
    def _():
        st_ref[...] = jnp.zeros_like(st_ref)

    xs = x_ref[...]
    bm, cm = bc_ref[:, 0:SSD_BC], bc_ref[:, SSD_BC:2 * SSD_BC]
    off = SM_DTB if rev else SM_DTF
    sm = sm_ref[...]
    dt = _softplus(sm + dtb_ref[0])
    a = -dt * jnp.exp(al_ref[0])
    dt_t = _softplus(sm.T + dtbc_ref[0])
    a_t = -dt_t * jnp.exp(alc_ref[0])
    tt, ss = _iota((L, L), 0), _iota((L, L), 1)
    causal = (ss >= tt) if rev else (ss <= tt)
    tri = jnp.where(causal, 1.0, 0.0)
    cum = _dot_exact_rhs(tri, a, 3)
    cum_t = _dot_exact_lhs(a_t, tri, 3, NT)
    last = 0 if rev else L - 1
    cum_last = cum[last:last + 1]
    expand = jnp.where(_iota((LANE, SSD_INNER), 0) - off == _iota((LANE, SSD_INNER), 1) // P, 1.0, 0.0)
    ecum = _dot_exact_lhs(jnp.exp(cum), expand, 2)
    wgt = _dot_exact_lhs(jnp.exp(cum_last - cum) * dt, expand, 2)

    st = st_ref[...]
    lane = _iota((1, LANE), 1)
    ys = []
    for g in range(SSD_GROUPS):
        cg, bg = cm[:, N * g:N * (g + 1)], bm[:, N * g:N * (g + 1)]
        cb = _dot(cg, bg, NT)
        for jp in range(SSD_HPG // 2):
            h0 = SSD_HPG * g + 2 * jp
            ms = []
            for h in (off + h0, off + h0 + 1):
                dec = jnp.exp(jnp.where(causal, cum[:, h:h + 1] - cum_t[h:h + 1, :], NEG))
                ms.append(cb * dec * dt_t[h:h + 1, :])
            xp = xs[:, P * h0:P * h0 + LANE]
            rhs = jnp.concatenate([jnp.where(lane < P, xp, 0.0), jnp.where(lane >= P, xp, 0.0)], axis=0)
            ys.append(_dot(jnp.concatenate(ms, axis=1), rhs))
    y = jnp.concatenate(ys, axis=1)
    y = y + jnp.concatenate([_dot(cm[:, N * g:N * (g + 1)], st[:, gw * g:gw * (g + 1)])
                             for g in range(SSD_GROUPS)], axis=1) * ecum
    xw = xs * wgt
    new = jnp.concatenate([_dot(bm[:, N * g:N * (g + 1)], xw[:, gw * g:gw * (g + 1)], TN)
                           for g in range(SSD_GROUPS)], axis=1)
    st_ref[...] = ecum[last:last + 1] * st + new

    if final:
        tot = (y + yf_ref[...] + d_ref[...] * xs) * _silu(z_ref[...])
        outs = []
        for g in range(SSD_GROUPS):
            tg = tot[:, gw * g:gw * (g + 1)]
            outs.append(tg * lax.rsqrt(jnp.mean(tg * tg, axis=-1, keepdims=True) + EPS))
        o_ref[...] = (jnp.concatenate(outs, axis=1) * gn_ref[...]).astype(o_ref.dtype)
    else:
        o_ref[...] = y


def _ssd_pass(p, xbc, dt_bias, a_log, bsz, nt_rows, n_ctx, rev, extra=None):
    L = SSD_CHUNK
    nc, nc_ctx = nt_rows // L, n_ctx // L

    def rows(width, col):
        return pl.BlockSpec((L, width), lambda b, j: (b * nc + _chunk_order(j, nc_ctx, nc, rev), col))

    d = 1 if rev else 0
    vec = pl.BlockSpec((1, 1, LANE), lambda b, j: (d, 0, 0))
    colv = pl.BlockSpec((1, LANE, 1), lambda b, j: (d, 0, 0))
    in_specs = [rows(SSD_INNER, 0), rows(2 * SSD_BC, SSD_INNER // (2 * SSD_BC)), rows(LANE, SMALL_OFF // LANE),
                vec, colv, vec, colv]

    def lanes(v):
        out = jnp.zeros((2, LANE), F32)
        return out.at[0, SM_DTF:SM_DTF + SSD_HEADS].set(v[0]).at[1, SM_DTB:SM_DTB + SSD_HEADS].set(v[1])

    dtb, alog = lanes(dt_bias), lanes(a_log)
    args = [xbc, xbc, p, dtb.reshape(2, 1, LANE), dtb.reshape(2, LANE, 1),
            alog.reshape(2, 1, LANE), alog.reshape(2, LANE, 1)]
    final = extra is not None
    if final:
        y_first, d_skip, gn = extra
        full = pl.BlockSpec((1, SSD_INNER), lambda b, j: (0, 0))
        in_specs += [rows(SSD_INNER, 0), rows(SSD_INNER, P_OFF['ssd_z'] // SSD_INNER), full, full]
        args += [y_first, p, jnp.repeat(d_skip, SSD_HD).reshape(1, SSD_INNER), gn.reshape(1, SSD_INNER)]
    return pl.pallas_call(
        functools.partial(_ssd_kernel, rev=rev, final=final),
        grid=(bsz, nc),
        in_specs=in_specs,
        out_specs=rows(SSD_INNER, 0),
        out_shape=jax.ShapeDtypeStruct((bsz * nt_rows, SSD_INNER), BF16 if final else F32),
        scratch_shapes=[pltpu.VMEM((SSD_STATE, SSD_INNER), F32)],
        compiler_params=_cparams(2),
        name="ssd_bwd_out" if final else "ssd_fwd",
    )(*args)


def _route(h2, rw_ref, rb_ref, run_ref):
    rows = h2.shape[0]
    scores = _sigmoid(_dot_hi(rw_ref[...], h2, NT))
    biased = scores + rb_ref[...]
    row = lambda a, e: a[e:e + 1, :]
    epg = EXPERTS_PER_GROUP
    best = gidx = None
    for g in range(N_EXPERT_GROUPS):
        vals = [row(biased, epg * g + i) for i in range(epg)]
        top2 = None
        for i in range(epg):
            for i2 in range(i + 1, epg):
                pair = vals[i] + vals[i2]
                top2 = pair if top2 is None else jnp.maximum(top2, pair)
        if g == 0:
            best, gidx = top2, jnp.zeros(top2.shape, jnp.int32)
        else:
            take = top2 > best
            best, gidx = jnp.where(take, top2, best), jnp.where(take, g, gidx)

    def pick(a, i):
        out = row(a, i)
        for g in range(1, N_EXPERT_GROUPS):
            out = jnp.where(gidx == g, row(a, epg * g + i), out)
        return out

    cand = [pick(biased, i) for i in range(epg)]
    raw = [pick(scores, i) for i in range(epg)]

    def argmax_first(vals, exclude):
        bv = bi = bs = None
        for i in range(epg):
            v = vals[i] if exclude is None else jnp.where(exclude == i, -jnp.inf, vals[i])
            if bv is None:
                bv, bi, bs = v, jnp.zeros(v.shape, jnp.int32), raw[0]
            else:
                take = v > bv
                bv, bi, bs = jnp.where(take, v, bv), jnp.where(take, i, bi), jnp.where(take, raw[i], bs)
        return bi, bs

    i0, s0 = argmax_first(cand, None)
    i1, s1 = argmax_first(cand, i0)
    e0, e1 = gidx * epg + i0, gidx * epg + i1
    tot = s0 + s1

    eid = _iota((N_EXPERTS, rows), 0)
    oh0 = jnp.where(eid == e0, 1.0, 0.0)
    oh1 = jnp.where(eid == e1, 1.0, 0.0)
    cnt = oh0 + oh1
    before = jnp.where(_iota((rows, rows), 0) < _iota((rows, rows), 1), 1.0, 0.0)
    prefix = _dot(cnt, before) + run_ref[:, 0:1]
    r0 = jnp.sum(oh0 * prefix, axis=0, keepdims=True).astype(jnp.int32)
    r1 = jnp.sum(oh1 * prefix, axis=0, keepdims=True).astype(jnp.int32)
    run_ref[...] = run_ref[...] + jnp.sum(cnt, axis=1, keepdims=True)
    return (e0, e1), (s0 / tot, s1 / tot), (r0, r1)


def _merge_kernel(yg_ref, ya_ref, ys_ref, mg_ref, x_ref, gb_ref, g1_ref, sh2_ref, sc2_ref, n2_ref,
                  wa_ref, wb_ref, wc_ref, wo_ref, rw_ref, rb_ref,
                  xo_ref, hx_ref, e_ref, g_ref, r_ref, c_ref, run_ref):
    d = D_MODEL
    rows = x_ref.shape[0]
    tile = pl.program_id(0) * pl.num_programs(1) + pl.program_id(1)

    @pl.when(tile == 0)
    def _():
        run_ref[...] = jnp.zeros_like(run_ref)

    gates = _sigmoid(mg_ref[...] + gb_ref[...])
    merged = (gates[:, 0:d] * _dot(yg_ref[...], wa_ref[...])
              + gates[:, d:2 * d] * _dot(ya_ref[...], wb_ref[...])
              + gates[:, 2 * d:3 * d] * _dot(ys_ref[...], wc_ref[...]))
    xn = x_ref[...] + g1_ref[0] * _dot(merged, wo_ref[...])
    xo_ref[...] = xn
    h2 = _normmod(xn, n2_ref[...], sh2_ref[0], sc2_ref[0])
    (e0, e1), (g0, g1), (r0, r1) = _route(h2, rw_ref, rb_ref, run_ref)
    e_ref[...] = jnp.concatenate([e0, e1], axis=0)
    g_ref[...] = jnp.concatenate([g0, g1], axis=0)
    r_ref[...] = jnp.concatenate([r0, r1], axis=0)
    c_ref[...] = run_ref[...].astype(jnp.int32)
    token1 = (tile * rows + _iota((rows, 1), 0) + 1).astype(F32)
    first_expert = jnp.broadcast_to(e0.astype(F32), (ID_LANES, rows)).T
    hx_ref[:, 0:d] = h2
    hx_ref[:, d:d + ID_LANES] = jnp.where(_iota((1, ID_LANES), 1) == 0, token1, first_expert)


def _merge(y_gla, y_att, y_ssd, p, xc, gate_b, mod, norm2_g, wa, wb, wc, wo, router_w, router_b, bsz, nt_rows, nct):
    tiles = nt_rows // ROW_TILE
    d = D_MODEL
    t = xc.shape[0]

    def rows(width, col=0):
        return pl.BlockSpec((ROW_TILE, width), lambda b, j: (b * tiles + j, col))

    def whole(a):
        return pl.BlockSpec(a.shape, lambda b, j: (0,) * a.ndim)

    gb = gate_b.reshape(1, N_BRANCH * d)
    n2 = norm2_g.reshape(1, d)
    rw, rb = router_w.T, router_b.reshape(N_EXPERTS, 1)
    lanes = pl.BlockSpec((2, ROW_TILE), lambda b, j: (0, b * tiles + j))
    return pl.pallas_call(
        _merge_kernel,
        grid=(bsz, tiles),
        in_specs=[rows(GLA_V), rows(ATT_Q), rows(SSD_INNER), rows(N_BRANCH * d, P_OFF['merge'] // (N_BRANCH * d)),
                  rows(d), whole(gb), _mod_spec(2, nct), _mod_spec(3, nct), _mod_spec(4, nct), whole(n2),
                  whole(wa), whole(wb), whole(wc), whole(wo), whole(rw), whole(rb)],
        out_specs=[rows(d), rows(H_EXT), lanes, lanes, lanes, pl.BlockSpec((N_EXPERTS, LANE), lambda b, j: (0, 0))],
        out_shape=[jax.ShapeDtypeStruct((t, d), F32), jax.ShapeDtypeStruct((t, H_EXT), F32),
                   jax.ShapeDtypeStruct((2, t), jnp.int32), jax.ShapeDtypeStruct((2, t), F32),
                   jax.ShapeDtypeStruct((2, t), jnp.int32), jax.ShapeDtypeStruct((N_EXPERTS, LANE), jnp.int32)],
        scratch_shapes=[pltpu.VMEM((N_EXPERTS, LANE), F32)],
        compiler_params=_cparams(2),
        name="merge",
    )(y_gla, y_att, y_ssd, p, xc, gb, mod, mod, mod, n2, wa, wb, wc, wo, rw, rb)


def _dispatch_kernel(dest_ref, h_ref, buf_in_ref, buf_ref, sem):
    del buf_in_ref
    rows = h_ref.shape[0]

    def row_copy(r, k):
        return pltpu.make_async_copy(h_ref.at[pl.ds(r, 1)], buf_ref.at[pl.ds(dest_ref[k, r], 1)], sem)

    def start(r, carry):
        row_copy(r, 0).start()
        row_copy(r, 1).start()
        return carry

    lax.fori_loop(0, rows, start, 0, unroll=DMA_UNROLL)
    for _ in range(2):
        pltpu.make_async_copy(h_ref, buf_ref.at[pl.ds(0, rows)], sem).wait()


def _dispatch(hx, dest, n_slots):
    t, w = hx.shape
    return pl.pallas_call(
        _dispatch_kernel,
        grid=(t // ROW_TILE,),
        in_specs=[pl.BlockSpec((2, ROW_TILE), lambda i: (0, i), memory_space=pltpu.SMEM),
                  pl.BlockSpec((ROW_TILE, w), lambda i: (i, 0)),
                  pl.BlockSpec(memory_space=pl.ANY)],
        out_specs=pl.BlockSpec(memory_space=pl.ANY),
        out_shape=jax.ShapeDtypeStruct((n_slots, w), F32),
        scratch_shapes=[pltpu.SemaphoreType.DMA(())],
        input_output_aliases={2: 0},
        compiler_params=_cparams(1),
        name="moe_dispatch",
    )(dest, hx, jnp.zeros((n_slots, w), F32))


def _expert_kernel(be_ref, real_ref, x_ref, w1_ref, w3_ref, w2_ref, o2_ref, obuf, tvec, tsm, sem_t, sem_o, *,
                   n_rows2, n_slots):
    i = pl.program_id(0)
    rows = x_ref.shape[0]
    cur = i % 2
    prv = 1 - cur
    d = D_MODEL

    def send(slot, r):
        return pltpu.make_async_copy(obuf.at[slot, pl.ds(r, 1)], o2_ref.at[pl.ds(tsm[slot, 0, r], 1)], sem_o.at[slot])

    def send_row(slot, r, carry):
        send(slot, r).start()
        return carry

    def drain(slot):
        pltpu.make_async_copy(obuf.at[slot], o2_ref.at[pl.ds(0, rows)], sem_o.at[slot]).wait()

    @pl.when(i == 0)
    def _():
        obuf[...] = jnp.zeros_like(obuf)

        def spare(r, carry):
            tsm[0, 0, r] = n_slots + r
            tsm[1, 0, r] = n_slots + rows + r
            return carry

        lax.fori_loop(0, rows, spare, 0)
        lax.fori_loop(0, rows, functools.partial(send_row, 0), 0)

    drain(cur)
    for r in range(rows):
        send(prv, r).start()

    x = x_ref[:, 0:d].astype(BF16)
    hid = _silu(_dot(x, w1_ref[0])) * _dot(x, w3_ref[0])
    obuf[cur] = _dot(hid, w2_ref[0])

    origin = x_ref[:, d:d + ID_LANES].T
    token1 = origin[0:1].astype(jnp.int32)
    second = jnp.where(origin[1:2].astype(jnp.int32) == be_ref[i], 0, 1)
    empty_row = n_rows2 + (i * rows - real_ref[i]) + _iota((1, rows), 1)
    target = jnp.where(token1 > 0, 2 * (token1 - 1) + second, empty_row)
    tvec[...] = jnp.broadcast_to(target, tvec.shape)
    to_smem = pltpu.make_async_copy(tvec, tsm.at[cur], sem_t)
    to_smem.start()
    to_smem.wait()

    @pl.when(i == pl.num_programs(0) - 1)
    def _():
        lax.fori_loop(0, rows, functools.partial(send_row, cur), 0)
        drain(prv)
        drain(cur)


def _experts(buf, block_e, block_real, w1, w3, w2, n_rows2):
    n_slots, w = buf.shape
    d, ff = w1.shape[1], w1.shape[2]
    grid_spec = pltpu.PrefetchScalarGridSpec(
        num_scalar_prefetch=2,
        grid=(n_slots // MOE_BLOCK,),
        in_specs=[pl.BlockSpec((MOE_BLOCK, w), lambda i, be, br: (i, 0)),
                  pl.BlockSpec((1, d, ff), lambda i, be, br: (be[i], 0, 0)),
                  pl.BlockSpec((1, d, ff), lambda i, be, br: (be[i], 0, 0)),
                  pl.BlockSpec((1, ff, d), lambda i, be, br: (be[i], 0, 0))],
        out_specs=pl.BlockSpec(memory_space=pl.ANY),
        scratch_shapes=[pltpu.VMEM((2, MOE_BLOCK, d), F32), pltpu.VMEM((8, MOE_BLOCK), jnp.int32),
                        pltpu.SMEM((2, 8, MOE_BLOCK), jnp.int32), pltpu.SemaphoreType.DMA(()),
                        pltpu.SemaphoreType.DMA((2,))],
    )
    return pl.pallas_call(
        functools.partial(_expert_kernel, n_rows2=n_rows2, n_slots=n_slots),
        grid_spec=grid_spec,
        out_shape=jax.ShapeDtypeStruct((n_slots + 2 * MOE_BLOCK, d), F32),
        compiler_params=_cparams(1),
        name="moe_experts",
    )(block_e, block_real, buf, w1, w3, w2)


def _combine_kernel(o2_ref, g_ref, x_ref, g2_ref, *rest, with_next):
    d = D_MODEL
    y = g_ref[:, 0:1] * o2_ref[:, 0:d] + g_ref[:, 1:2] * o2_ref[:, d:2 * d]
    xo = x_ref[...] + g2_ref[0] * y
    if with_next:
        n1_ref, sh_ref, sc_ref, xo_ref, h_ref = rest
        h_ref[...] = _normmod(xo, n1_ref[...], sh_ref[0], sc_ref[0]).astype(h_ref.dtype)
    else:
        xo_ref, = rest
    xo_ref[...] = xo


def _combine(o2, gates_col, x_new, mod, bsz, nt_rows, nct, latent_only, nxt=None):
    tiles = nt_rows // ROW_TILE
    d = D_MODEL
    skip = nct if latent_only else 0
    out_tiles = tiles - skip
    src = lambda b, j: b * tiles + skip + j
    kind = lambda j: jnp.where(skip + j >= nct, 1, 0)
    pairs = o2.reshape(o2.shape[0] // 2, 2 * d)
    in_specs = [pl.BlockSpec((ROW_TILE, 2 * d), lambda b, j: (src(b, j), 0)),
                pl.BlockSpec((ROW_TILE, 2), lambda b, j: (src(b, j), 0)),
                pl.BlockSpec((ROW_TILE, d), lambda b, j: (src(b, j), 0)),
                pl.BlockSpec((1, 1, d), lambda b, j: (2 * b + kind(j), 0, 5))]
    args = [pairs, gates_col, x_new, mod]
    out_rows = pl.BlockSpec((ROW_TILE, d), lambda b, j: (b * out_tiles + j, 0))
    out_specs, out_shape = [out_rows], [jax.ShapeDtypeStruct((bsz * out_tiles * ROW_TILE, d), F32)]
    if nxt is not None:
        norm_g, mod_next = nxt
        in_specs += [pl.BlockSpec((1, d), lambda b, j: (0, 0)),
                     pl.BlockSpec((1, 1, d), lambda b, j: (2 * b + kind(j), 0, 0)),
                     pl.BlockSpec((1, 1, d), lambda b, j: (2 * b + kind(j), 0, 1))]
        args += [norm_g.reshape(1, d), mod_next, mod_next]
        out_specs.append(out_rows)
        out_shape.append(jax.ShapeDtypeStruct((bsz * out_tiles * ROW_TILE, d), BF16))
    return pl.pallas_call(
        functools.partial(_combine_kernel, with_next=nxt is not None),
        grid=(bsz, out_tiles),
        in_specs=in_specs,
        out_specs=out_specs,
        out_shape=out_shape,
        compiler_params=_cparams(2),
        name="moe_combine",
    )(*args)


def _moe(hx, routing, x_new, mod, w1, w3, w2, bsz, nt_rows, nct, latent_only, nxt):
    t = hx.shape[0]
    e_idx, gates, rank, counts = routing
    counts = counts[:, 0]
    padded = (counts + MOE_BLOCK - 1) // MOE_BLOCK * MOE_BLOCK
    pad_end = jnp.cumsum(padded)
    pad_start = pad_end - padded
    first = jnp.sum(jnp.where(e_idx[None] == jnp.arange(N_EXPERTS)[:, None, None], pad_start[:, None, None], 0),
                    axis=0)
    dest = first + rank
    n_blocks = -(-2 * t // MOE_BLOCK) + N_EXPERTS
    block_e = jnp.minimum(jnp.searchsorted(pad_end, jnp.arange(n_blocks, dtype=jnp.int32) * MOE_BLOCK,
                                           side='right'), N_EXPERTS - 1).astype(jnp.int32)
    block_real = jnp.sum(jnp.where(jnp.arange(N_EXPERTS)[None] <= block_e[:, None], counts[None], 0), axis=1)
    assert (2 * t) % MOE_BLOCK == 0
    buf = _dispatch(hx, dest, n_blocks * MOE_BLOCK)
    o2 = _experts(buf, block_e, block_real.astype(jnp.int32), w1, w3, w2, 2 * t)
    return _combine(o2, gates.T, x_new, mod, bsz, nt_rows, nct, latent_only, nxt)


def _permute_w_in(w_in):
    offsets = np.concatenate([[0], np.cumsum(IN_SPLITS)])
    start = dict(zip(IN_NAMES, offsets[:-1]))
    width = dict(zip(IN_NAMES, IN_SPLITS))
    cols = [w_in[:, start[n]:start[n] + width[n]] for n in P_ORDER]
    cols.append(jnp.zeros((w_in.shape[0], SMALL_PAD), w_in.dtype))
    return jnp.concatenate(cols, axis=1).astype(BF16)


def _layer(xc, h, mod, lp, router_w, router_b, rope, bsz, nt_rows, n_ctx, nxt):
    nct = n_ctx // ROW_TILE
    if h is None:
        h = _norm_modulate(xc, lp['norm1_g'], mod, bsz, nt_rows, nct)
    p = _in_proj(h, _permute_w_in(lp['w_in']))

    o_f = _gla_pass(p, lp['gla_w2'], lp['gla_b2'], bsz, nt_rows, n_ctx, rev=False)
    y_gla = _gla_pass(p, lp['gla_w2'], lp['gla_b2'], bsz, nt_rows, n_ctx, rev=True,
                      extra=(o_f, lp['gla_norm_g']))
    y_att = _attention(p, rope[0], rope[1], lp['att_sink'], lp['q_norm_g'], lp['k_norm_g'], bsz, nt_rows, n_ctx)
    xbc = _ssd_conv(p, lp['ssd_conv_w'], lp['ssd_conv_b'], bsz, nt_rows, n_ctx)
    y_f = _ssd_pass(p, xbc, lp['ssd_dt_bias'], lp['ssd_a_log'], bsz, nt_rows, n_ctx, rev=False)
    y_ssd = _ssd_pass(p, xbc, lp['ssd_dt_bias'], lp['ssd_a_log'], bsz, nt_rows, n_ctx, rev=True,
                      extra=(y_f, lp['ssd_d'], lp['ssd_norm_g']))

    x_new, hx, *routing = _merge(y_gla, y_att, y_ssd, p, xc, lp['gate_b'], mod, lp['norm2_g'],
                                 lp['w_br_a'].astype(BF16), lp['w_br_b'].astype(BF16), lp['w_br_c'].astype(BF16),
                                 lp['w_out'].astype(BF16), router_w, router_b, bsz, nt_rows, nct)
    return _moe(hx, routing, x_new, mod, lp['w1'].astype(BF16), lp['w3'].astype(BF16), lp['w2'].astype(BF16),
                bsz, nt_rows, nct, latent_only=nxt is None, nxt=nxt)


def kernel(x, c, ctx, c_ctx, w_mod, b_mod, norm1_g, w_in, gla_w2, gla_b2, gla_norm_g, q_norm_g, k_norm_g, att_sink, ssd_conv_w, ssd_conv_b, ssd_dt_bias, ssd_a_log, ssd_d, ssd_norm_g, gate_b, w_br_a, w_br_b, w_br_c, w_out, norm2_g, router_w, router_b, w1, w3, w2):
    bsz, seq, d = x.shape
    n_ctx = ctx.shape[1]
    nt_rows = n_ctx + seq
    assert d == D_MODEL and n_ctx % ROW_TILE == 0 and seq % ROW_TILE == 0 and nt_rows % n_ctx == 0
    assert seq % GRID_W == 0 and bsz < 16

    cc = jnp.zeros((16, d), F32).at[:bsz].set(c).at[bsz].set(c_ctx)
    mods = _mod_vectors(cc, w_mod, b_mod)
    rope = _rope_tables(seq)
    xc = jnp.concatenate([ctx, x], axis=1).reshape(bsz * nt_rows, d)
    params = dict(norm1_g=norm1_g, w_in=w_in, gla_w2=gla_w2, gla_b2=gla_b2, gla_norm_g=gla_norm_g,
                  q_norm_g=q_norm_g, k_norm_g=k_norm_g, att_sink=att_sink, ssd_conv_w=ssd_conv_w,
                  ssd_conv_b=ssd_conv_b, ssd_dt_bias=ssd_dt_bias, ssd_a_log=ssd_a_log, ssd_d=ssd_d,
                  ssd_norm_g=ssd_norm_g, gate_b=gate_b, w_br_a=w_br_a, w_br_b=w_br_b, w_br_c=w_br_c,
                  w_out=w_out, norm2_g=norm2_g, w1=w1, w3=w3, w2=w2)

    def mod_rows(l):
        m = mods[l]
        return jnp.stack([jnp.broadcast_to(m[bsz], (bsz, N_MOD * d)), m[:bsz]], axis=1).reshape(2 * bsz, 1, N_MOD * d)

    h = None
    for l in range(DEPTH):
        lp = {name: val[l] for name, val in params.items()}
        nxt = (norm1_g[l + 1], mod_rows(l + 1)) if l + 1 < DEPTH else None
        out = _layer(xc, h, mod_rows(l), lp, router_w, router_b, rope, bsz, nt_rows, n_ctx, nxt)
        xc, h = (out[0], out[1]) if nxt is not None else (out[0], None)
    return xc.reshape(bsz, seq, d)
```

```python
import functools

import numpy as np
import jax
import jax.numpy as jnp
from jax import lax
from jax.experimental import pallas as pl
from jax.experimental.pallas import tpu as pltpu

F32 = jnp.float32
BF16 = jnp.bfloat16

D_MODEL = 1024
DEPTH = 2
GRID_W = 64
EPS = 1e-6
N_MOD = 6

GLA_HEADS = 4
GLA_DK = 64
GLA_DV = 128
GLA_GATE_RANK = 16
GLA_GATE_TEMP = 16.0
GLA_CHUNK = 64
GLA_SUB = 8
GLA_STEP = 4
GLA_QK = GLA_HEADS * GLA_DK
GLA_V = GLA_HEADS * GLA_DV

ATT_Q_HEADS = 8
ATT_KV_HEADS = 2
ATT_HD = 64
ATT_WINDOW = 128
ATT_BLOCK = 128
ROPE_BASE = 10000.0
ATT_Q = ATT_Q_HEADS * ATT_HD
ATT_KV = ATT_KV_HEADS * ATT_HD

SSD_HEADS = 16
SSD_HD = 64
SSD_GROUPS = 2
SSD_STATE = 128
SSD_CONV = 5
SSD_CHUNK = 128
SSD_INNER = SSD_HEADS * SSD_HD
SSD_BC = SSD_GROUPS * SSD_STATE
SSD_XBC = SSD_INNER + 2 * SSD_BC
SSD_HPG = SSD_HEADS // SSD_GROUPS

N_BRANCH = 3
N_EXPERTS = 16
N_EXPERT_GROUPS = 4
EXPERTS_PER_GROUP = N_EXPERTS // N_EXPERT_GROUPS
EXPERT_FF = 1024

IN_NAMES = ('gla_q', 'gla_k', 'gla_v', 'gla_r', 'gla_gf', 'gla_gb', 'att_q', 'att_k', 'att_v',
            'ssd_z', 'ssd_xbc', 'ssd_dtf', 'ssd_dtb', 'merge')
IN_SPLITS = (GLA_QK, GLA_QK, GLA_V, GLA_V, GLA_GATE_RANK, GLA_GATE_RANK, ATT_Q, ATT_KV, ATT_KV,
             SSD_INNER, SSD_XBC, SSD_HEADS, SSD_HEADS, N_BRANCH * D_MODEL)

P_ORDER = ('merge', 'ssd_z', 'ssd_xbc', 'gla_v', 'gla_r', 'att_q', 'gla_q', 'gla_k', 'att_k', 'att_v',
           'gla_gf', 'gla_gb', 'ssd_dtf', 'ssd_dtb')
LANE = 128
ID_LANES = LANE
SMALL_PAD = 4 * 16
N_P = sum(IN_SPLITS) + SMALL_PAD
H_EXT = D_MODEL + ID_LANES


def _p_offsets():
    width = dict(zip(IN_NAMES, IN_SPLITS))
    off, o = {}, 0
    for name in P_ORDER:
        off[name] = o
        o += width[name]
    return off


P_OFF = _p_offsets()
SMALL_OFF = P_OFF['gla_gf']
SM_GF, SM_GB, SM_DTF, SM_DTB = 0, 16, 32, 48

ROW_TILE = 256
MOE_BLOCK = 256
NEG = -1e30
LOG2E = 1.4426950408889634
DMA_UNROLL = 8
VMEM_LIMIT = 56 * 1024 * 1024

NN = (((1,), (0,)), ((), ()))
NT = (((1,), (1,)), ((), ()))
TN = (((0,), (0,)), ((), ()))


def _dot(a, b, dims=NN):
    return lax.dot_general(a.astype(BF16), b.astype(BF16), dims, preferred_element_type=F32)


def _dot_hi(a, b, dims=NN):
    return lax.dot_general(a.astype(F32), b.astype(F32), dims, precision=lax.Precision.HIGHEST,
                           preferred_element_type=F32)


def _split(x, terms):
    parts, rem = [], x
    for i in range(terms):
        part = rem.astype(BF16)
        parts.append(part)
        if i + 1 < terms:
            rem = rem - part.astype(F32)
    return parts


def _dot_exact_rhs(m01, x, terms, dims=NN):
    m01 = m01.astype(BF16)
    return sum(lax.dot_general(m01, part, dims, preferred_element_type=F32) for part in _split(x, terms))


def _dot_exact_lhs(x, m01, terms, dims=NN):
    m01 = m01.astype(BF16)
    return sum(lax.dot_general(part, m01, dims, preferred_element_type=F32) for part in _split(x, terms))


def _dot3(a, b, dims=NN):
    (ah, al), (bh, bl) = _split(a, 2), _split(b, 2)
    mm = lambda x, y: lax.dot_general(x, y, dims, preferred_element_type=F32)
    return mm(ah, bh) + mm(ah, bl) + mm(al, bh)


def _sigmoid(x):
    return 1.0 / (1.0 + jnp.exp(-x))


def _silu(x):
    return x * _sigmoid(x)


def _softplus(x):
    return jnp.maximum(x, 0.0) + jnp.log(1.0 + jnp.exp(-jnp.abs(x)))


def _iota(shape, dim):
    return lax.broadcasted_iota(jnp.int32, shape, dim)


def _cparams(n_axes):
    return pltpu.CompilerParams(dimension_semantics=("arbitrary",) * n_axes, vmem_limit_bytes=VMEM_LIMIT)


def _largest_tile(n, cap, mult):
    t = (min(cap, n) // mult) * mult
    while n % t:
        t -= mult
    return t


def _mod_kernel(c_ref, w_ref, b_ref, o_ref):
    o_ref[0] = _dot_hi(_silu(c_ref[...]), w_ref[0]) + b_ref[0]


def _mod_vectors(cc, w_mod, b_mod):
    n_l, d, n6 = w_mod.shape
    tn = 1024
    return pl.pallas_call(
        _mod_kernel,
        grid=(n_l, n6 // tn),
        in_specs=[pl.BlockSpec(cc.shape, lambda l, j: (0, 0)),
                  pl.BlockSpec((1, d, tn), lambda l, j: (l, 0, j)),
                  pl.BlockSpec((1, 1, tn), lambda l, j: (l, 0, j))],
        out_specs=pl.BlockSpec((1, cc.shape[0], tn), lambda l, j: (l, 0, j)),
        out_shape=jax.ShapeDtypeStruct((n_l, cc.shape[0], n6), F32),
        compiler_params=_cparams(2),
        name="mod_vectors",
    )(cc, w_mod, b_mod.reshape(n_l, 1, n6))


def _mod_spec(which, nct):
    return pl.BlockSpec((1, 1, D_MODEL), lambda b, j: (2 * b + jnp.where(j >= nct, 1, 0), 0, which))


def _normmod(x, g, shift, scale):
    y = x * lax.rsqrt(jnp.mean(x * x, axis=-1, keepdims=True) + EPS) * g
    return y * (1.0 + scale) + shift


def _normmod_kernel(x_ref, g_ref, sh_ref, sc_ref, o_ref):
    o_ref[...] = _normmod(x_ref[...], g_ref[...], sh_ref[0], sc_ref[0]).astype(o_ref.dtype)


def _norm_modulate(xc, g, mod, bsz, nt_rows, nct):
    tiles = nt_rows // ROW_TILE
    return pl.pallas_call(
        _normmod_kernel,
        grid=(bsz, tiles),
        in_specs=[pl.BlockSpec((ROW_TILE, D_MODEL), lambda b, j: (b * tiles + j, 0)),
                  pl.BlockSpec((1, D_MODEL), lambda b, j: (0, 0)),
                  _mod_spec(0, nct), _mod_spec(1, nct)],
        out_specs=pl.BlockSpec((ROW_TILE, D_MODEL), lambda b, j: (b * tiles + j, 0)),
        out_shape=jax.ShapeDtypeStruct(xc.shape, BF16),
        compiler_params=_cparams(2),
        name="norm_modulate",
    )(xc, g.reshape(1, -1), mod, mod)


def _matmul_kernel(a_ref, w_ref, o_ref):
    o_ref[...] = jnp.dot(a_ref[...], w_ref[...], preferred_element_type=F32).astype(o_ref.dtype)


def _in_proj(h, w):
    m, kdim = h.shape
    n = w.shape[1]
    tm = _largest_tile(m, 2048, 256)
    tn = 1152
    return pl.pallas_call(
        _matmul_kernel,
        grid=(m // tm, n // tn),
        in_specs=[pl.BlockSpec((tm, kdim), lambda i, j: (i, 0)),
                  pl.BlockSpec((kdim, tn), lambda i, j: (0, j))],
        out_specs=pl.BlockSpec((tm, tn), lambda i, j: (i, j)),
        out_shape=jax.ShapeDtypeStruct((m, n), F32),
        compiler_params=_cparams(2),
        name="in_proj",
    )(h, w)


def _chunk_order(j, nc_ctx, nc, rev):
    if not rev:
        return j
    return jnp.where(j < nc_ctx, nc_ctx - 1 - j, nc - 1 - (j - nc_ctx))


def _gla_kernel(q_ref, k_ref, v_ref, sm_ref, w2_ref, b2_ref, *rest, rev, final):
    if final:
        of_ref, r_ref, gn_ref, o_ref, st_ref, b_ref = rest
    else:
        o_ref, st_ref, b_ref = rest
    L, S = GLA_CHUNK, GLA_SUB
    rows = q_ref.shape[0]

    @pl.when(pl.program_id(1) == 0)
    def _():
        st_ref[...] = jnp.zeros_like(st_ref)

    logits = _dot3(sm_ref[...], w2_ref[0]) + b2_ref[0]
    g = (jnp.minimum(logits, 0.0) - jnp.log(1.0 + jnp.exp(-jnp.abs(logits)))) * (LOG2E / GLA_GATE_TEMP)
    rr, cc = _iota((rows, rows), 0), _iota((rows, rows), 1)
    tri = jnp.where((rr // L == cc // L) & ((cc >= rr) if rev else (cc <= rr)), 1.0, 0.0)
    b_ref[...] = _dot_exact_rhs(tri, g, 3)

    head_of_lane = _iota((1, GLA_QK), 1) // GLA_DK
    seg = jnp.where(_iota((GLA_QK, GLA_V), 0) // GLA_DK == _iota((GLA_QK, GLA_V), 1) // GLA_DV, 1.0, 0.0)
    sub_row = _iota((S, 1), 0)
    by_head = lambda x: [jnp.where(head_of_lane == h, x, 0.0) for h in range(GLA_HEADS)]
    n_sub = L // S
    order = list(reversed(range(rows // L))) if rev else list(range(rows // L))
    last = 0 if rev else L - 1

    groups = []
    for i in range(L // (2 * S)):
        r0 = 2 * S * i
        if (not rev) and i > 0:
            groups.append((r0, 2 * S, 0, r0, r0 - 1))
        if rev and r0 + 2 * S < L:
            groups.append((r0, 2 * S, r0 + 2 * S, L - r0 - 2 * S, r0 + 2 * S))
        groups.append((r0, S, r0 + S, S, r0 + S) if rev else (r0 + S, S, r0, S, r0 + S - 1))

    vals = {c: (q_ref[L * c:L * (c + 1)] * (GLA_DK ** -0.5), k_ref[L * c:L * (c + 1)],
                v_ref[L * c:L * (c + 1)], b_ref[L * c:L * (c + 1)]) for c in order}

    scores = {}
    for c in order:
        q, k, v, b = vals[c]
        for gi, (q0, nq, k0, nk, edge) in enumerate(groups):
            ref = b[edge:edge + 1]
            qd = q[q0:q0 + nq] * jnp.exp2(b[q0:q0 + nq] - ref)
            kd = k[k0:k0 + nk] * jnp.exp2(ref - b[k0:k0 + nk])
            scores[c, gi] = _dot(jnp.concatenate(by_head(qd), axis=0), kd, NT)

    ps = []
    for c in order:
        q, k, v, b = vals[c]
        for i in range(n_sub):
            qi, ki, bi = q[S * i:S * (i + 1)], k[S * i:S * (i + 1)], b[S * i:S * (i + 1)]
            for s in range(S):
                valid = (sub_row <= s) if rev else (sub_row >= s)
                ps.append(qi * jnp.exp2(jnp.where(valid, bi - bi[s:s + 1], NEG)) * ki[s:s + 1])
    w_all = _dot(jnp.concatenate(ps, axis=0), seg)

    intra = {}
    for ci, c in enumerate(order):
        q, k, v, b = vals[c]
        pieces = [None] * n_sub

        def add(i, val):
            pieces[i] = val if pieces[i] is None else pieces[i] + val

        for gi, (q0, nq, k0, nk, edge) in enumerate(groups):
            s_g = scores[c, gi]
            val = jnp.concatenate([_dot(s_g[nq * h:nq * (h + 1)], v[k0:k0 + nk, GLA_DV * h:GLA_DV * (h + 1)])
                                   for h in range(GLA_HEADS)], axis=1)
            for j in range(nq // S):
                add(q0 // S + j, val[S * j:S * (j + 1)])
        for i in range(n_sub):
            vi = v[S * i:S * (i + 1)]
            base = (ci * n_sub + i) * S * S
            for s in range(S):
                add(i, w_all[base + S * s:base + S * (s + 1)] * vi[s:s + 1])
        intra[c] = jnp.concatenate(pieces, axis=0)

    st = st_ref[...]
    for c in order:
        q, k, v, b = vals[c]
        b_last = b[last:last + 1]
        o = intra[c] + jnp.concatenate([_dot(qh, st, NT) for qh in by_head(q * jnp.exp2(b))], axis=1)
        kk = k * jnp.exp2(b_last - b)
        new = None
        for h in range(GLA_HEADS):
            u = _dot(v[:, GLA_DV * h:GLA_DV * (h + 1)], kk, TN)
            new = u if new is None else jnp.where(head_of_lane == h, u, new)
        st = jnp.exp2(b_last) * st + new
        if final:
            tot = o + of_ref[L * c:L * (c + 1)]
            outs = []
            for h in range(GLA_HEADS):
                oh = tot[:, GLA_DV * h:GLA_DV * (h + 1)]
                outs.append(oh * lax.rsqrt(jnp.mean(oh * oh, axis=-1, keepdims=True) + EPS) * gn_ref[...])
            o_ref[L * c:L * (c + 1)] = (jnp.concatenate(outs, axis=1) * _silu(r_ref[L * c:L * (c + 1)])).astype(o_ref.dtype)
        else:
            o_ref[L * c:L * (c + 1)] = o
    st_ref[...] = st


def _gla_pass(p, w2, b2, bsz, nt_rows, n_ctx, rev, extra=None):
    rows_per_step = GLA_STEP * GLA_CHUNK
    nb, nb_ctx = nt_rows // rows_per_step, n_ctx // rows_per_step

    def rows(width, col):
        return pl.BlockSpec((rows_per_step, width), lambda b, j: (b * nb + _chunk_order(j, nb_ctx, nb, rev), col))

    d = 1 if rev else 0
    in_specs = [rows(GLA_QK, P_OFF['gla_q'] // GLA_QK), rows(GLA_QK, P_OFF['gla_k'] // GLA_QK),
                rows(GLA_V, P_OFF['gla_v'] // GLA_V), rows(LANE, SMALL_OFF // LANE),
                pl.BlockSpec((1, LANE, GLA_QK), lambda b, j: (d, 0, 0)),
                pl.BlockSpec((1, 1, GLA_QK), lambda b, j: (d, 0, 0))]
    w2_lanes = jnp.zeros((2, LANE, GLA_QK), F32)
    w2_lanes = w2_lanes.at[0, SM_GF:SM_GF + GLA_GATE_RANK].set(w2[0]).at[1, SM_GB:SM_GB + GLA_GATE_RANK].set(w2[1])
    args = [p, p, p, p, w2_lanes, b2.reshape(2, 1, GLA_QK)]
    final = extra is not None
    if final:
        o_first, gn = extra
        in_specs += [rows(GLA_V, 0), rows(GLA_V, P_OFF['gla_r'] // GLA_V),
                     pl.BlockSpec((1, GLA_DV), lambda b, j: (0, 0))]
        args += [o_first, p, gn.reshape(1, GLA_DV)]
    return pl.pallas_call(
        functools.partial(_gla_kernel, rev=rev, final=final),
        grid=(bsz, nb),
        in_specs=in_specs,
        out_specs=rows(GLA_V, 0),
        out_shape=jax.ShapeDtypeStruct((bsz * nt_rows, GLA_V), BF16 if final else F32),
        scratch_shapes=[pltpu.VMEM((GLA_DV, GLA_QK), F32), pltpu.VMEM((rows_per_step, GLA_QK), F32)],
        compiler_params=_cparams(2),
        name="gla_bwd_out" if final else "gla_fwd",
    )(*args)


def _att_kernel(sink_ref, q_ref, kp_ref, kc_ref, kn_ref, vp_ref, vc_ref, vn_ref,
                cp_ref, sp_ref, cc_ref, sc_ref, cn_ref, sn_ref, kx_ref, vx_ref, qg_ref, kg_ref,
                o_ref, *, nb_ctx, nb_lat, n_ctx):
    j = pl.program_id(1)
    blk = ATT_BLOCK
    lane = _iota((1, LANE), 1)
    half_mean = jnp.where(_iota((LANE, LANE), 0) // ATT_HD == _iota((LANE, LANE), 1) // ATT_HD,
                          1.0 / ATT_HD, 0.0)

    def norm(x, g_ref):
        return x * lax.rsqrt(_dot(x * x, half_mean) + EPS) * g_ref[...]

    def rope(x, c_ref, s_ref):
        swapped = jnp.where(lane % 32 < 16, pltpu.roll(x, LANE - 16, 1), pltpu.roll(x, 16, 1))
        return x * c_ref[...] + swapped * s_ref[...]

    def attend(qpairs, keys, vals, bias):
        keys_sw = pltpu.roll(keys, ATT_HD, 1).astype(BF16)
        vals_sw = pltpu.roll(vals, ATT_HD, 1).astype(BF16)
        keys, vals = keys.astype(BF16), vals.astype(BF16)
        rep = ATT_Q_HEADS // ATT_KV_HEADS
        heads = range(ATT_Q_HEADS)
        straight = [(h % 2) == (h // rep) for h in heads]
        scores = []
        for h in heads:
            qm = jnp.where((lane >= ATT_HD) if h % 2 else (lane < ATT_HD), qpairs[h // 2], 0.0)
            s = _dot(qm, keys if straight[h] else keys_sw, NT)
            if bias is not None:
                for c0, add in bias:
                    parts = [s[:, :c0], s[:, c0:c0 + blk] + add, s[:, c0 + blk:]]
                    s = jnp.concatenate([part for part in parts if part.shape[1]], axis=1)
            scores.append(s)
        probs, dens = [], []
        for h in heads:
            sk = sink_ref[0, h] * LOG2E
            m = jnp.maximum(jnp.max(scores[h], axis=-1, keepdims=True), sk)
            pr = jnp.exp2(scores[h] - m)
            probs.append(pr)
            dens.append(jnp.sum(pr, axis=-1, keepdims=True) + jnp.exp2(sk - m))
        outs = [_dot(probs[h], vals if straight[h] else vals_sw) / dens[h] for h in heads]
        o_ref[...] = jnp.concatenate([jnp.where(lane < ATT_HD, outs[2 * pi], outs[2 * pi + 1])
                                      for pi in range(ATT_Q_HEADS // 2)], axis=1).astype(o_ref.dtype)

    kx = norm(kx_ref[...], kg_ref)
    vx = vx_ref[...]
    q_scale = ATT_HD ** -0.5 * LOG2E

    @pl.when(j < nb_ctx)
    def _():
        qpairs = [norm(q_ref[:, LANE * pi:LANE * (pi + 1)], qg_ref) * q_scale
                  for pi in range(ATT_Q_HEADS // 2)]
        attend(qpairs, kx, vx, None)

    @pl.when(j >= nb_ctx)
    def _():
        li = j - nb_ctx
        qpairs = [rope(norm(q_ref[:, LANE * pi:LANE * (pi + 1)], qg_ref), cc_ref, sc_ref) * q_scale
                  for pi in range(ATT_Q_HEADS // 2)]
        keys = jnp.concatenate([kx, rope(norm(kp_ref[...], kg_ref), cp_ref, sp_ref),
                                rope(norm(kc_ref[...], kg_ref), cc_ref, sc_ref),
                                rope(norm(kn_ref[...], kg_ref), cn_ref, sn_ref)], axis=0)
        vals = jnp.concatenate([vx, vp_ref[...], vc_ref[...], vn_ref[...]], axis=0)
        a, jj = _iota((blk, blk), 0), _iota((blk, blk), 1)
        prev_ok = (jj >= a) & (li > 0)
        next_ok = (jj <= a) & (li < nb_lat - 1)
        bias = [(n_ctx, jnp.where(prev_ok, 0.0, NEG)), (n_ctx + 2 * blk, jnp.where(next_ok, 0.0, NEG))]
        attend(qpairs, keys, vals, bias)


def _rope_tables(n):
    rows = n // GRID_W
    row = jnp.repeat(jnp.arange(rows), GRID_W).astype(F32)
    col = jnp.tile(jnp.arange(GRID_W), rows).astype(F32)
    axis_dim = ATT_HD // 2
    inv_freq = ROPE_BASE ** (-jnp.arange(0, axis_dim, 2, dtype=F32) / axis_dim)
    ang_r, ang_c = row[:, None] * inv_freq, col[:, None] * inv_freq
    cos = jnp.concatenate([jnp.cos(ang_r), jnp.cos(ang_r), jnp.cos(ang_c), jnp.cos(ang_c)], axis=1)
    sin = jnp.concatenate([-jnp.sin(ang_r), jnp.sin(ang_r), -jnp.sin(ang_c), jnp.sin(ang_c)], axis=1)
    return jnp.tile(cos, (1, LANE // ATT_HD)), jnp.tile(sin, (1, LANE // ATT_HD))


def _attention(p, cos, sin, sink, qg, kg, bsz, nt_rows, n_ctx):
    blk = ATT_BLOCK
    nb, nb_ctx = nt_rows // blk, n_ctx // blk
    nb_lat = nb - nb_ctx
    k_col, v_col = P_OFF['att_k'] // LANE, P_OFF['att_v'] // LANE

    def lat(j, d):
        return jnp.clip(j - nb_ctx + d, 0, nb_lat - 1)

    def kv(col, d):
        return pl.BlockSpec((blk, LANE), lambda b, j: (b * nb + nb_ctx + lat(j, d), col))

    def tab(d):
        return pl.BlockSpec((blk, LANE), lambda b, j: (lat(j, d), 0))

    ctx_rows = nt_rows // n_ctx
    in_specs = [pl.BlockSpec(memory_space=pltpu.SMEM),
                pl.BlockSpec((blk, ATT_Q), lambda b, j: (b * nb + j, P_OFF['att_q'] // ATT_Q)),
                kv(k_col, -1), kv(k_col, 0), kv(k_col, 1), kv(v_col, -1), kv(v_col, 0), kv(v_col, 1),
                tab(-1), tab(-1), tab(0), tab(0), tab(1), tab(1),
                pl.BlockSpec((n_ctx, LANE), lambda b, j: (b * ctx_rows, k_col)),
                pl.BlockSpec((n_ctx, LANE), lambda b, j: (b * ctx_rows, v_col)),
                pl.BlockSpec((1, LANE), lambda b, j: (0, 0)),
                pl.BlockSpec((1, LANE), lambda b, j: (0, 0))]
    tile2 = lambda g: jnp.tile(g.reshape(1, ATT_HD), (1, LANE // ATT_HD))
    return pl.pallas_call(
        functools.partial(_att_kernel, nb_ctx=nb_ctx, nb_lat=nb_lat, n_ctx=n_ctx),
        grid=(bsz, nb),
        in_specs=in_specs,
        out_specs=pl.BlockSpec((blk, ATT_Q), lambda b, j: (b * nb + j, 0)),
        out_shape=jax.ShapeDtypeStruct((bsz * nt_rows, ATT_Q), BF16),
        compiler_params=_cparams(2),
        name="attention",
    )(sink.reshape(1, ATT_Q_HEADS), p, p, p, p, p, p, p, cos, sin, cos, sin, cos, sin, p, p, tile2(qg), tile2(kg))


def _conv_kernel(xp_ref, xc_ref, xn_ref, w_ref, b_ref, o_ref, ext_ref, *, nct, nt):
    j = pl.program_id(1)
    rows = xc_ref.shape[0]
    no_prev = (j == 0) | (j == nct)
    no_next = (j == nct - 1) | (j == nt - 1)
    ext_ref[0:8] = jnp.where(no_prev, 0.0, xp_ref[...])
    ext_ref[8:8 + rows] = xc_ref[...]
    ext_ref[8 + rows:16 + rows] = jnp.where(no_next, 0.0, xn_ref[...])
    pad = SSD_CONV // 2
    acc = b_ref[...] + w_ref[0:1] * ext_ref[pl.ds(8 - pad, rows)]
    for t in range(1, SSD_CONV):
        acc = acc + w_ref[t:t + 1] * ext_ref[pl.ds(8 - pad + t, rows)]
    o_ref[...] = _silu(acc)


def _ssd_conv(p, w, bias, bsz, nt_rows, n_ctx):
    tiles, nct = nt_rows // ROW_TILE, n_ctx // ROW_TILE
    ct = 512
    c0 = P_OFF['ssd_xbc'] // ct
    r8 = ROW_TILE // 8
    last8 = bsz * nt_rows // 8 - 1
    return pl.pallas_call(
        functools.partial(_conv_kernel, nct=nct, nt=tiles),
        grid=(bsz, tiles, SSD_XBC // ct),
        in_specs=[pl.BlockSpec((8, ct), lambda b, j, c: (jnp.maximum((b * tiles + j) * r8 - 1, 0), c0 + c)),
                  pl.BlockSpec((ROW_TILE, ct), lambda b, j, c: (b * tiles + j, c0 + c)),
                  pl.BlockSpec((8, ct), lambda b, j, c: (jnp.minimum((b * tiles + j + 1) * r8, last8), c0 + c)),
                  pl.BlockSpec((SSD_CONV, ct), lambda b, j, c: (0, c)),
                  pl.BlockSpec((1, ct), lambda b, j, c: (0, c))],
        out_specs=pl.BlockSpec((ROW_TILE, ct), lambda b, j, c: (b * tiles + j, c)),
        out_shape=jax.ShapeDtypeStruct((bsz * nt_rows, SSD_XBC), F32),
        scratch_shapes=[pltpu.VMEM((ROW_TILE + 16, ct), F32)],
        compiler_params=_cparams(3),
        name="ssd_conv",
    )(p, p, p, w, bias.reshape(1, -1))


def _ssd_kernel(x_ref, bc_ref, sm_ref, dtb_ref, dtbc_ref, al_ref, alc_ref, *rest, rev, final):
    if final:
        yf_ref, z_ref, d_ref, gn_ref, o_ref, st_ref = rest
    else:
        o_ref, st_ref = rest
    L, P, N = SSD_CHUNK, SSD_HD, SSD_STATE
    gw = SSD_HPG * P

    @pl.when(pl.program_id(1) == 0)
    def _():
        st_ref[...] = jnp.zeros_like(st_ref)

    xs = x_ref[...]
    bm, cm = bc_ref[:, 0:SSD_BC], bc_ref[:, SSD_BC:2 * SSD_BC]
    off = SM_DTB if rev else SM_DTF
    sm = sm_ref[...]
    dt = _softplus(sm + dtb_ref[0])
    a = -dt * jnp.exp(al_ref[0])
    dt_t = _softplus(sm.T + dtbc_ref[0])
    a_t = -dt_t * jnp.exp(alc_ref[0])
    tt, ss = _iota((L, L), 0), _iota((L, L), 1)
    causal = (ss >= tt) if rev else (ss <= tt)
    tri = jnp.where(causal, 1.0, 0.0)
    cum = _dot_exact_rhs(tri, a, 2)
    cum_t = _dot_exact_lhs(a_t, tri, 2, NT)
    last = 0 if rev else L - 1
    cum_last = cum[last:last + 1]
    expand = jnp.where(_iota((LANE, SSD_INNER), 0) - off == _iota((LANE, SSD_INNER), 1) // P, 1.0, 0.0)
    ecum = _dot(jnp.exp(cum), expand)
    wgt = _dot(jnp.exp(cum_last - cum) * dt, expand)

    st = st_ref[...]
    lane = _iota((1, LANE), 1)
    ys = []
    for g in range(SSD_GROUPS):
        cg, bg = cm[:, N * g:N * (g + 1)], bm[:, N * g:N * (g + 1)]
        cb = _dot(cg, bg, NT)
        for jp in range(SSD_HPG // 2):
            h0 = SSD_HPG * g + 2 * jp
            ms = []
            for h in (off + h0, off + h0 + 1):
                dec = jnp.exp(jnp.where(causal, cum[:, h:h + 1] - cum_t[h:h + 1, :], NEG))
                ms.append(cb * dec * dt_t[h:h + 1, :])
            xp = xs[:, P * h0:P * h0 + LANE]
            rhs = jnp.concatenate([jnp.where(lane < P, xp, 0.0), jnp.where(lane >= P, xp, 0.0)], axis=0)
            ys.append(_dot(jnp.concatenate(ms, axis=1), rhs))
    y = jnp.concatenate(ys, axis=1)
    y = y + jnp.concatenate([_dot(cm[:, N * g:N * (g + 1)], st[:, gw * g:gw * (g + 1)])
                             for g in range(SSD_GROUPS)], axis=1) * ecum
    xw = xs * wgt
    new = jnp.concatenate([_dot(bm[:, N * g:N * (g + 1)], xw[:, gw * g:gw * (g + 1)], TN)
                           for g in range(SSD_GROUPS)], axis=1)
    st_ref[...] = ecum[last:last + 1] * st + new

    if final:
        tot = (y + yf_ref[...] + d_ref[...] * xs) * _silu(z_ref[...])
        outs = []
        for g in range(SSD_GROUPS):
            tg = tot[:, gw * g:gw * (g + 1)]
            outs.append(tg * lax.rsqrt(jnp.mean(tg * tg, axis=-1, keepdims=True) + EPS))
        o_ref[...] = (jnp.concatenate(outs, axis=1) * gn_ref[...]).astype(o_ref.dtype)
    else:
        o_ref[...] = y


def _ssd_pass(p, xbc, dt_bias, a_log, bsz, nt_rows, n_ctx, rev, extra=None):
    L = SSD_CHUNK
    nc, nc_ctx = nt_rows // L, n_ctx // L

    def rows(width, col):
        return pl.BlockSpec((L, width), lambda b, j: (b * nc + _chunk_order(j, nc_ctx, nc, rev), col))

    d = 1 if rev else 0
    vec = pl.BlockSpec((1, 1, LANE), lambda b, j: (d, 0, 0))
    colv = pl.BlockSpec((1, LANE, 1), lambda b, j: (d, 0, 0))
    in_specs = [rows(SSD_INNER, 0), rows(2 * SSD_BC, SSD_INNER // (2 * SSD_BC)), rows(LANE, SMALL_OFF // LANE),
                vec, colv, vec, colv]

    def lanes(v):
        out = jnp.zeros((2, LANE), F32)
        return out.at[0, SM_DTF:SM_DTF + SSD_HEADS].set(v[0]).at[1, SM_DTB:SM_DTB + SSD_HEADS].set(v[1])

    dtb, alog = lanes(dt_bias), lanes(a_log)
    args = [xbc, xbc, p, dtb.reshape(2, 1, LANE), dtb.reshape(2, LANE, 1),
            alog.reshape(2, 1, LANE), alog.reshape(2, LANE, 1)]
    final = extra is not None
    if final:
        y_first, d_skip, gn = extra
        full = pl.BlockSpec((1, SSD_INNER), lambda b, j: (0, 0))
        in_specs += [rows(SSD_INNER, 0), rows(SSD_INNER, P_OFF['ssd_z'] // SSD_INNER), full, full]
        args += [y_first, p, jnp.repeat(d_skip, SSD_HD).reshape(1, SSD_INNER), gn.reshape(1, SSD_INNER)]
    return pl.pallas_call(
        functools.partial(_ssd_kernel, rev=rev, final=final),
        grid=(bsz, nc),
        in_specs=in_specs,
        out_specs=rows(SSD_INNER, 0),
        out_shape=jax.ShapeDtypeStruct((bsz * nt_rows, SSD_INNER), BF16 if final else F32),
        scratch_shapes=[pltpu.VMEM((SSD_STATE, SSD_INNER), F32)],
        compiler_params=_cparams(2),
        name="ssd_bwd_out" if final else "ssd_fwd",
    )(*args)


def _route(h2, rw_ref, rb_ref, run_ref):
    rows = h2.shape[0]
    scores = _sigmoid(_dot3(h2, rw_ref[...]).T[0:N_EXPERTS])
    biased = scores + rb_ref[...]
    row = lambda a, e: a[e:e + 1, :]
    epg = EXPERTS_PER_GROUP
    best = gidx = None
    for g in range(N_EXPERT_GROUPS):
        vals = [row(biased, epg * g + i) for i in range(epg)]
        top2 = None
        for i in range(epg):
            for i2 in range(i + 1, epg):
                pair = vals[i] + vals[i2]
                top2 = pair if top2 is None else jnp.maximum(top2, pair)
        if g == 0:
            best, gidx = top2, jnp.zeros(top2.shape, jnp.int32)
        else:
            take = top2 > best
            best, gidx = jnp.where(take, top2, best), jnp.where(take, g, gidx)

    def pick(a, i):
        out = row(a, i)
        for g in range(1, N_EXPERT_GROUPS):
            out = jnp.where(gidx == g, row(a, epg * g + i), out)
        return out

    cand = [pick(biased, i) for i in range(epg)]
    raw = [pick(scores, i) for i in range(epg)]

    def argmax_first(vals, exclude):
        bv = bi = bs = None
        for i in range(epg):
            v = vals[i] if exclude is None else jnp.where(exclude == i, -jnp.inf, vals[i])
            if bv is None:
                bv, bi, bs = v, jnp.zeros(v.shape, jnp.int32), raw[0]
            else:
                take = v > bv
                bv, bi, bs = jnp.where(take, v, bv), jnp.where(take, i, bi), jnp.where(take, raw[i], bs)
        return bi, bs

    i0, s0 = argmax_first(cand, None)
    i1, s1 = argmax_first(cand, i0)
    e0, e1 = gidx * epg + i0, gidx * epg + i1
    tot = s0 + s1

    eid = _iota((N_EXPERTS, rows), 0)
    oh0 = jnp.where(eid == e0, 1.0, 0.0)
    oh1 = jnp.where(eid == e1, 1.0, 0.0)
    cnt = oh0 + oh1
    before = jnp.where(_iota((rows, rows), 0) < _iota((rows, rows), 1), 1.0, 0.0)
    prefix = _dot(cnt, before) + run_ref[:, 0:1]
    r0 = jnp.sum(oh0 * prefix, axis=0, keepdims=True).astype(jnp.int32)
    r1 = jnp.sum(oh1 * prefix, axis=0, keepdims=True).astype(jnp.int32)
    run_ref[...] = run_ref[...] + jnp.sum(cnt, axis=1, keepdims=True)
    return (e0, e1), (s0 / tot, s1 / tot), (r0, r1)


def _merge_kernel(yg_ref, ya_ref, ys_ref, mg_ref, x_ref, gb_ref, g1_ref, sh2_ref, sc2_ref, n2_ref,
                  wa_ref, wb_ref, wc_ref, wo_ref, rw_ref, rb_ref,
                  xo_ref, hx_ref, e_ref, g_ref, r_ref, c_ref, run_ref):
    d = D_MODEL
    rows = x_ref.shape[0]
    tile = pl.program_id(0) * pl.num_programs(1) + pl.program_id(1)

    @pl.when(tile == 0)
    def _():
        run_ref[...] = jnp.zeros_like(run_ref)

    gates = _sigmoid(mg_ref[...] + gb_ref[...])
    merged = (gates[:, 0:d] * _dot(yg_ref[...], wa_ref[...])
              + gates[:, d:2 * d] * _dot(ya_ref[...], wb_ref[...])
              + gates[:, 2 * d:3 * d] * _dot(ys_ref[...], wc_ref[...]))
    xn = x_ref[...] + g1_ref[0] * _dot(merged, wo_ref[...])
    xo_ref[...] = xn
    h2 = _normmod(xn, n2_ref[...], sh2_ref[0], sc2_ref[0])
    (e0, e1), (g0, g1), (r0, r1) = _route(h2, rw_ref, rb_ref, run_ref)
    e_ref[...] = jnp.concatenate([e0, e1], axis=0)
    g_ref[...] = jnp.concatenate([g0, g1], axis=0)
    r_ref[...] = jnp.concatenate([r0, r1], axis=0)
    c_ref[...] = run_ref[...].astype(jnp.int32)
    token1 = (tile * rows + _iota((rows, 1), 0) + 1).astype(F32)
    first_expert = jnp.broadcast_to(e0.astype(F32), (ID_LANES, rows)).T
    hx_ref[:, 0:d] = h2
    hx_ref[:, d:d + ID_LANES] = jnp.where(_iota((1, ID_LANES), 1) == 0, token1, first_expert)


def _merge(y_gla, y_att, y_ssd, p, xc, gate_b, mod, norm2_g, wa, wb, wc, wo, router_w, router_b, bsz, nt_rows, nct):
    tiles = nt_rows // ROW_TILE
    d = D_MODEL
    t = xc.shape[0]

    def rows(width, col=0):
        return pl.BlockSpec((ROW_TILE, width), lambda b, j: (b * tiles + j, col))

    def whole(a):
        return pl.BlockSpec(a.shape, lambda b, j: (0,) * a.ndim)

    gb = gate_b.reshape(1, N_BRANCH * d)
    n2 = norm2_g.reshape(1, d)
    rw = jnp.pad(router_w, ((0, 0), (0, LANE - N_EXPERTS)))
    rb = router_b.reshape(N_EXPERTS, 1)
    lanes = pl.BlockSpec((2, ROW_TILE), lambda b, j: (0, b * tiles + j))
    return pl.pallas_call(
        _merge_kernel,
        grid=(bsz, tiles),
        in_specs=[rows(GLA_V), rows(ATT_Q), rows(SSD_INNER), rows(N_BRANCH * d, P_OFF['merge'] // (N_BRANCH * d)),
                  rows(d), whole(gb), _mod_spec(2, nct), _mod_spec(3, nct), _mod_spec(4, nct), whole(n2),
                  whole(wa), whole(wb), whole(wc), whole(wo), whole(rw), whole(rb)],
        out_specs=[rows(d), rows(H_EXT), lanes, lanes, lanes, pl.BlockSpec((N_EXPERTS, LANE), lambda b, j: (0, 0))],
        out_shape=[jax.ShapeDtypeStruct((t, d), F32), jax.ShapeDtypeStruct((t, H_EXT), F32),
                   jax.ShapeDtypeStruct((2, t), jnp.int32), jax.ShapeDtypeStruct((2, t), F32),
                   jax.ShapeDtypeStruct((2, t), jnp.int32), jax.ShapeDtypeStruct((N_EXPERTS, LANE), jnp.int32)],
        scratch_shapes=[pltpu.VMEM((N_EXPERTS, LANE), F32)],
        compiler_params=_cparams(2),
        name="merge",
    )(y_gla, y_att, y_ssd, p, xc, gb, mod, mod, mod, n2, wa, wb, wc, wo, rw, rb)


def _dispatch_kernel(dest_ref, h_ref, buf_in_ref, buf_ref, sem):
    del buf_in_ref
    rows = h_ref.shape[0]

    def row_copy(r, k):
        return pltpu.make_async_copy(h_ref.at[pl.ds(r, 1)], buf_ref.at[pl.ds(dest_ref[k, r], 1)], sem)

    def start(r, carry):
        row_copy(r, 0).start()
        row_copy(r, 1).start()
        return carry

    lax.fori_loop(0, rows, start, 0, unroll=DMA_UNROLL)
    for _ in range(2):
        pltpu.make_async_copy(h_ref, buf_ref.at[pl.ds(0, rows)], sem).wait()


def _dispatch(hx, dest, n_slots):
    t, w = hx.shape
    return pl.pallas_call(
        _dispatch_kernel,
        grid=(t // ROW_TILE,),
        in_specs=[pl.BlockSpec((2, ROW_TILE), lambda i: (0, i), memory_space=pltpu.SMEM),
                  pl.BlockSpec((ROW_TILE, w), lambda i: (i, 0)),
                  pl.BlockSpec(memory_space=pl.ANY)],
        out_specs=pl.BlockSpec(memory_space=pl.ANY),
        out_shape=jax.ShapeDtypeStruct((n_slots, w), F32),
        scratch_shapes=[pltpu.SemaphoreType.DMA(())],
        input_output_aliases={2: 0},
        compiler_params=_cparams(1),
        name="moe_dispatch",
    )(dest, hx, jnp.zeros((n_slots, w), F32))


def _expert_kernel(be_ref, real_ref, x_ref, w1_ref, w3_ref, w2_ref, o2_ref, obuf, tvec, tsm, sem_t, sem_o, *,
                   n_rows2, n_slots):
    i = pl.program_id(0)
    rows = x_ref.shape[0]
    cur = i % 2
    prv = 1 - cur
    d = D_MODEL

    def send(slot, r):
        return pltpu.make_async_copy(obuf.at[slot, pl.ds(r, 1)], o2_ref.at[pl.ds(tsm[slot, 0, r], 1)], sem_o.at[slot])

    def send_row(slot, r, carry):
        send(slot, r).start()
        return carry

    def drain(slot):
        pltpu.make_async_copy(obuf.at[slot], o2_ref.at[pl.ds(0, rows)], sem_o.at[slot]).wait()

    @pl.when(i == 0)
    def _():
        obuf[...] = jnp.zeros_like(obuf)

        def spare(r, carry):
            tsm[0, 0, r] = n_slots + r
            tsm[1, 0, r] = n_slots + rows + r
            return carry

        lax.fori_loop(0, rows, spare, 0)
        lax.fori_loop(0, rows, functools.partial(send_row, 0), 0)

    drain(cur)

    origin = x_ref[:, d:d + ID_LANES].T
    token1 = origin[0:1].astype(jnp.int32)
    plane = jnp.where(origin[1:2].astype(jnp.int32) == be_ref[i], 0, n_rows2 // 2)
    empty_row = n_rows2 + (i * rows - real_ref[i]) + _iota((1, rows), 1)
    tvec[...] = jnp.broadcast_to(jnp.where(token1 > 0, plane + token1 - 1, empty_row), tvec.shape)
    to_smem = pltpu.make_async_copy(tvec, tsm.at[cur], sem_t)
    to_smem.start()

    for r in range(rows):
        send(prv, r).start()

    x = x_ref[:, 0:d].astype(BF16)
    hid = _silu(_dot(x, w1_ref[0])) * _dot(x, w3_ref[0])
    obuf[cur] = _dot(hid, w2_ref[0])
    to_smem.wait()

    @pl.when(i == pl.num_programs(0) - 1)
    def _():
        lax.fori_loop(0, rows, functools.partial(send_row, cur), 0)
        drain(prv)
        drain(cur)


def _experts(buf, block_e, block_real, w1, w3, w2, n_rows2):
    n_slots, w = buf.shape
    d, ff = w1.shape[1], w1.shape[2]
    grid_spec = pltpu.PrefetchScalarGridSpec(
        num_scalar_prefetch=2,
        grid=(n_slots // MOE_BLOCK,),
        in_specs=[pl.BlockSpec((MOE_BLOCK, w), lambda i, be, br: (i, 0)),
                  pl.BlockSpec((1, d, ff), lambda i, be, br: (be[i], 0, 0)),
                  pl.BlockSpec((1, d, ff), lambda i, be, br: (be[i], 0, 0)),
                  pl.BlockSpec((1, ff, d), lambda i, be, br: (be[i], 0, 0))],
        out_specs=pl.BlockSpec(memory_space=pl.ANY),
        scratch_shapes=[pltpu.VMEM((2, MOE_BLOCK, d), F32), pltpu.VMEM((8, MOE_BLOCK), jnp.int32),
                        pltpu.SMEM((2, 8, MOE_BLOCK), jnp.int32), pltpu.SemaphoreType.DMA(()),
                        pltpu.SemaphoreType.DMA((2,))],
    )
    return pl.pallas_call(
        functools.partial(_expert_kernel, n_rows2=n_rows2, n_slots=n_slots),
        grid_spec=grid_spec,
        out_shape=jax.ShapeDtypeStruct((n_slots + 2 * MOE_BLOCK, d), F32),
        compiler_params=_cparams(1),
        name="moe_experts",
    )(block_e, block_real, buf, w1, w3, w2)


def _combine_kernel(oa_ref, ob_ref, g_ref, x_ref, g2_ref, *rest, with_next):
    y = g_ref[:, 0:1] * oa_ref[...] + g_ref[:, 1:2] * ob_ref[...]
    xo = x_ref[...] + g2_ref[0] * y
    if with_next:
        n1_ref, sh_ref, sc_ref, xo_ref, h_ref = rest
        h_ref[...] = _normmod(xo, n1_ref[...], sh_ref[0], sc_ref[0]).astype(h_ref.dtype)
    else:
        xo_ref, = rest
    xo_ref[...] = xo


def _combine(o2, gates_col, x_new, mod, bsz, nt_rows, nct, latent_only, nxt=None):
    tiles = nt_rows // ROW_TILE
    d = D_MODEL
    skip = nct if latent_only else 0
    out_tiles = tiles - skip
    src = lambda b, j: b * tiles + skip + j
    kind = lambda j: jnp.where(skip + j >= nct, 1, 0)
    second = bsz * tiles
    in_specs = [pl.BlockSpec((ROW_TILE, d), lambda b, j: (src(b, j), 0)),
                pl.BlockSpec((ROW_TILE, d), lambda b, j: (second + src(b, j), 0)),
                pl.BlockSpec((ROW_TILE, 2), lambda b, j: (src(b, j), 0)),
                pl.BlockSpec((ROW_TILE, d), lambda b, j: (src(b, j), 0)),
                pl.BlockSpec((1, 1, d), lambda b, j: (2 * b + kind(j), 0, 5))]
    args = [o2, o2, gates_col, x_new, mod]
    out_rows = pl.BlockSpec((ROW_TILE, d), lambda b, j: (b * out_tiles + j, 0))
    out_specs, out_shape = [out_rows], [jax.ShapeDtypeStruct((bsz * out_tiles * ROW_TILE, d), F32)]
    if nxt is not None:
        norm_g, mod_next = nxt
        in_specs += [pl.BlockSpec((1, d), lambda b, j: (0, 0)),
                     pl.BlockSpec((1, 1, d), lambda b, j: (2 * b + kind(j), 0, 0)),
                     pl.BlockSpec((1, 1, d), lambda b, j: (2 * b + kind(j), 0, 1))]
        args += [norm_g.reshape(1, d), mod_next, mod_next]
        out_specs.append(out_rows)
        out_shape.append(jax.ShapeDtypeStruct((bsz * out_tiles * ROW_TILE, d), BF16))
    return pl.pallas_call(
        functools.partial(_combine_kernel, with_next=nxt is not None),
        grid=(bsz, out_tiles),
        in_specs=in_specs,
        out_specs=out_specs,
        out_shape=out_shape,
        compiler_params=_cparams(2),
        name="moe_combine",
    )(*args)


def _moe(hx, routing, x_new, mod, w1, w3, w2, bsz, nt_rows, nct, latent_only, nxt):
    t = hx.shape[0]
    e_idx, gates, rank, counts = routing
    counts = counts[:, 0]
    padded = (counts + MOE_BLOCK - 1) // MOE_BLOCK * MOE_BLOCK
    pad_end = jnp.cumsum(padded)
    pad_start = pad_end - padded
    first = jnp.sum(jnp.where(e_idx[None] == jnp.arange(N_EXPERTS)[:, None, None], pad_start[:, None, None], 0),
                    axis=0)
    dest = first + rank
    n_blocks = -(-2 * t // MOE_BLOCK) + N_EXPERTS
    block_e = jnp.minimum(jnp.searchsorted(pad_end, jnp.arange(n_blocks, dtype=jnp.int32) * MOE_BLOCK,
                                           side='right'), N_EXPERTS - 1).astype(jnp.int32)
    block_real = jnp.sum(jnp.where(jnp.arange(N_EXPERTS)[None] <= block_e[:, None], counts[None], 0), axis=1)
    assert (2 * t) % MOE_BLOCK == 0
    buf = _dispatch(hx, dest, n_blocks * MOE_BLOCK)
    o2 = _experts(buf, block_e, block_real.astype(jnp.int32), w1, w3, w2, 2 * t)
    return _combine(o2, gates.T, x_new, mod, bsz, nt_rows, nct, latent_only, nxt)


def _permute_w_in(w_in):
    offsets = np.concatenate([[0], np.cumsum(IN_SPLITS)])
    start = dict(zip(IN_NAMES, offsets[:-1]))
    width = dict(zip(IN_NAMES, IN_SPLITS))
    cols = [w_in[:, start[n]:start[n] + width[n]] for n in P_ORDER]
    cols.append(jnp.zeros((w_in.shape[0], SMALL_PAD), w_in.dtype))
    return jnp.concatenate(cols, axis=1).astype(BF16)


def _layer(xc, h, mod, lp, router_w, router_b, rope, bsz, nt_rows, n_ctx, nxt):
    nct = n_ctx // ROW_TILE
    if h is None:
        h = _norm_modulate(xc, lp['norm1_g'], mod, bsz, nt_rows, nct)
    p = _in_proj(h, _permute_w_in(lp['w_in']))

    o_f = _gla_pass(p, lp['gla_w2'], lp['gla_b2'], bsz, nt_rows, n_ctx, rev=False)
    y_gla = _gla_pass(p, lp['gla_w2'], lp['gla_b2'], bsz, nt_rows, n_ctx, rev=True,
                      extra=(o_f, lp['gla_norm_g']))
    y_att = _attention(p, rope[0], rope[1], lp['att_sink'], lp['q_norm_g'], lp['k_norm_g'], bsz, nt_rows, n_ctx)
    xbc = _ssd_conv(p, lp['ssd_conv_w'], lp['ssd_conv_b'], bsz, nt_rows, n_ctx)
    y_f = _ssd_pass(p, xbc, lp['ssd_dt_bias'], lp['ssd_a_log'], bsz, nt_rows, n_ctx, rev=False)
    y_ssd = _ssd_pass(p, xbc, lp['ssd_dt_bias'], lp['ssd_a_log'], bsz, nt_rows, n_ctx, rev=True,
                      extra=(y_f, lp['ssd_d'], lp['ssd_norm_g']))

    x_new, hx, *routing = _merge(y_gla, y_att, y_ssd, p, xc, lp['gate_b'], mod, lp['norm2_g'],
                                 lp['w_br_a'].astype(BF16), lp['w_br_b'].astype(BF16), lp['w_br_c'].astype(BF16),
                                 lp['w_out'].astype(BF16), router_w, router_b, bsz, nt_rows, nct)
    return _moe(hx, routing, x_new, mod, lp['w1'].astype(BF16), lp['w3'].astype(BF16), lp['w2'].astype(BF16),
                bsz, nt_rows, nct, latent_only=nxt is None, nxt=nxt)


def kernel(x, c, ctx, c_ctx, w_mod, b_mod, norm1_g, w_in, gla_w2, gla_b2, gla_norm_g, q_norm_g, k_norm_g, att_sink, ssd_conv_w, ssd_conv_b, ssd_dt_bias, ssd_a_log, ssd_d, ssd_norm_g, gate_b, w_br_a, w_br_b, w_br_c, w_out, norm2_g, router_w, router_b, w1, w3, w2):
    bsz, seq, d = x.shape
    n_ctx = ctx.shape[1]
    nt_rows = n_ctx + seq
    assert d == D_MODEL and n_ctx % ROW_TILE == 0 and seq % ROW_TILE == 0 and nt_rows % n_ctx == 0
    assert seq % GRID_W == 0 and bsz < 16

    cc = jnp.zeros((16, d), F32).at[:bsz].set(c).at[bsz].set(c_ctx)
    mods = _mod_vectors(cc, w_mod, b_mod)
    rope = _rope_tables(seq)
    xc = jnp.concatenate([ctx, x], axis=1).reshape(bsz * nt_rows, d)
    params = dict(norm1_g=norm1_g, w_in=w_in, gla_w2=gla_w2, gla_b2=gla_b2, gla_norm_g=gla_norm_g,
                  q_norm_g=q_norm_g, k_norm_g=k_norm_g, att_sink=att_sink, ssd_conv_w=ssd_conv_w,
                  ssd_conv_b=ssd_conv_b, ssd_dt_bias=ssd_dt_bias, ssd_a_log=ssd_a_log, ssd_d=ssd_d,
                  ssd_norm_g=ssd_norm_g, gate_b=gate_b, w_br_a=w_br_a, w_br_b=w_br_b, w_br_c=w_br_c,
                  w_out=w_out, norm2_g=norm2_g, w1=w1, w3=w3, w2=w2)

    def mod_rows(l):
        m = mods[l]
        return jnp.stack([jnp.broadcast_to(m[bsz], (bsz, N_MOD * d)), m[:bsz]], axis=1).reshape(2 * bsz, 1, N_MOD * d)

    h = None
    for l in range(DEPTH):
        lp = {name: val[l] for name, val in params.items()}
        nxt = (norm1_g[l + 1], mod_rows(l + 1)) if l + 1 < DEPTH else None
        out = _layer(xc, h, mod_rows(l), lp, router_w, router_b, rope, bsz, nt_rows, n_ctx, nxt)
        xc, h = (out[0], out[1]) if nxt is not None else (out[0], None)
    return xc.reshape(bsz, seq, d)
```

```python
import functools

import numpy as np
import jax
import jax.numpy as jnp
from jax import lax
from jax.experimental import pallas as pl
from jax.experimental.pallas import tpu as pltpu

F32 = jnp.float32
BF16 = jnp.bfloat16

D_MODEL = 1024
DEPTH = 2
GRID_W = 64
EPS = 1e-6
N_MOD = 6

GLA_HEADS = 4
GLA_DK = 64
GLA_DV = 128
GLA_GATE_RANK = 16
GLA_GATE_TEMP = 16.0
GLA_CHUNK = 64
GLA_SUB = 8
GLA_STEP = 4
GLA_QK = GLA_HEADS * GLA_DK
GLA_V = GLA_HEADS * GLA_DV

ATT_Q_HEADS = 8
ATT_KV_HEADS = 2
ATT_HD = 64
ATT_WINDOW = 128
ATT_BLOCK = 128
ROPE_BASE = 10000.0
ATT_Q = ATT_Q_HEADS * ATT_HD
ATT_KV = ATT_KV_HEADS * ATT_HD

SSD_HEADS = 16
SSD_HD = 64
SSD_GROUPS = 2
SSD_STATE = 128
SSD_CONV = 5
SSD_CHUNK = 128
SSD_INNER = SSD_HEADS * SSD_HD
SSD_BC = SSD_GROUPS * SSD_STATE
SSD_XBC = SSD_INNER + 2 * SSD_BC
SSD_HPG = SSD_HEADS // SSD_GROUPS

N_BRANCH = 3
N_EXPERTS = 16
N_EXPERT_GROUPS = 4
EXPERTS_PER_GROUP = N_EXPERTS // N_EXPERT_GROUPS
EXPERT_FF = 1024

IN_NAMES = ('gla_q', 'gla_k', 'gla_v', 'gla_r', 'gla_gf', 'gla_gb', 'att_q', 'att_k', 'att_v',
            'ssd_z', 'ssd_xbc', 'ssd_dtf', 'ssd_dtb', 'merge')
IN_SPLITS = (GLA_QK, GLA_QK, GLA_V, GLA_V, GLA_GATE_RANK, GLA_GATE_RANK, ATT_Q, ATT_KV, ATT_KV,
             SSD_INNER, SSD_XBC, SSD_HEADS, SSD_HEADS, N_BRANCH * D_MODEL)

P_ORDER = ('merge', 'ssd_z', 'ssd_xbc', 'gla_v', 'gla_r', 'att_q', 'gla_q', 'gla_k', 'att_k', 'att_v',
           'gla_gf', 'gla_gb', 'ssd_dtf', 'ssd_dtb')
LANE = 128
ID_LANES = LANE
SMALL_PAD = 4 * 16
N_P = sum(IN_SPLITS) + SMALL_PAD
H_EXT = D_MODEL + ID_LANES


def _p_offsets():
    width = dict(zip(IN_NAMES, IN_SPLITS))
    off, o = {}, 0
    for name in P_ORDER:
        off[name] = o
        o += width[name]
    return off


P_OFF = _p_offsets()
SMALL_OFF = P_OFF['gla_gf']
SM_GF, SM_GB, SM_DTF, SM_DTB = 0, 16, 32, 48

ROW_TILE = 256
MOE_BLOCK = 256
NEG = -1e30
LOG2E = 1.4426950408889634
DMA_UNROLL = 8
VMEM_LIMIT = 56 * 1024 * 1024

NN = (((1,), (0,)), ((), ()))
NT = (((1,), (1,)), ((), ()))
TN = (((0,), (0,)), ((), ()))


def _dot(a, b, dims=NN):
    return lax.dot_general(a.astype(BF16), b.astype(BF16), dims, preferred_element_type=F32)


def _dot_hi(a, b, dims=NN):
    return lax.dot_general(a.astype(F32), b.astype(F32), dims, precision=lax.Precision.HIGHEST,
                           preferred_element_type=F32)


def _split(x, terms):
    parts, rem = [], x
    for i in range(terms):
        part = rem.astype(BF16)
        parts.append(part)
        if i + 1 < terms:
            rem = rem - part.astype(F32)
    return parts


def _dot_exact_rhs(m01, x, terms, dims=NN):
    m01 = m01.astype(BF16)
    return sum(lax.dot_general(m01, part, dims, preferred_element_type=F32) for part in _split(x, terms))


def _dot_exact_lhs(x, m01, terms, dims=NN):
    m01 = m01.astype(BF16)
    return sum(lax.dot_general(part, m01, dims, preferred_element_type=F32) for part in _split(x, terms))


def _dot3(a, b, dims=NN):
    (ah, al), (bh, bl) = _split(a, 2), _split(b, 2)
    mm = lambda x, y: lax.dot_general(x, y, dims, preferred_element_type=F32)
    return mm(ah, bh) + mm(ah, bl) + mm(al, bh)


def _sigmoid(x):
    return 1.0 / (1.0 + jnp.exp(-x))


def _silu(x):
    return x * _sigmoid(x)


def _softplus(x):
    return jnp.maximum(x, 0.0) + jnp.log(1.0 + jnp.exp(-jnp.abs(x)))


def _iota(shape, dim):
    return lax.broadcasted_iota(jnp.int32, shape, dim)


def _cparams(n_axes):
    return pltpu.CompilerParams(dimension_semantics=("arbitrary",) * n_axes, vmem_limit_bytes=VMEM_LIMIT)


def _largest_tile(n, cap, mult):
    t = (min(cap, n) // mult) * mult
    while n % t:
        t -= mult
    return t


def _mod_kernel(c_ref, w_ref, b_ref, o_ref):
    o_ref[0] = _dot_hi(_silu(c_ref[...]), w_ref[0]) + b_ref[0]


def _mod_vectors(cc, w_mod, b_mod):
    n_l, d, n6 = w_mod.shape
    tn = 1024
    return pl.pallas_call(
        _mod_kernel,
        grid=(n_l, n6 // tn),
        in_specs=[pl.BlockSpec(cc.shape, lambda l, j: (0, 0)),
                  pl.BlockSpec((1, d, tn), lambda l, j: (l, 0, j)),
                  pl.BlockSpec((1, 1, tn), lambda l, j: (l, 0, j))],
        out_specs=pl.BlockSpec((1, cc.shape[0], tn), lambda l, j: (l, 0, j)),
        out_shape=jax.ShapeDtypeStruct((n_l, cc.shape[0], n6), F32),
        compiler_params=_cparams(2),
        name="mod_vectors",
    )(cc, w_mod, b_mod.reshape(n_l, 1, n6))


def _mod_spec(which, nct):
    return pl.BlockSpec((1, 1, D_MODEL), lambda b, j: (2 * b + jnp.where(j >= nct, 1, 0), 0, which))


def _normmod(x, g, shift, scale):
    y = x * lax.rsqrt(jnp.mean(x * x, axis=-1, keepdims=True) + EPS) * g
    return y * (1.0 + scale) + shift


def _normmod_kernel(x_ref, g_ref, sh_ref, sc_ref, o_ref):
    o_ref[...] = _normmod(x_ref[...], g_ref[...], sh_ref[0], sc_ref[0]).astype(o_ref.dtype)


def _norm_modulate(xc, g, mod, bsz, nt_rows, nct):
    tiles = nt_rows // ROW_TILE
    return pl.pallas_call(
        _normmod_kernel,
        grid=(bsz, tiles),
        in_specs=[pl.BlockSpec((ROW_TILE, D_MODEL), lambda b, j: (b * tiles + j, 0)),
                  pl.BlockSpec((1, D_MODEL), lambda b, j: (0, 0)),
                  _mod_spec(0, nct), _mod_spec(1, nct)],
        out_specs=pl.BlockSpec((ROW_TILE, D_MODEL), lambda b, j: (b * tiles + j, 0)),
        out_shape=jax.ShapeDtypeStruct(xc.shape, BF16),
        compiler_params=_cparams(2),
        name="norm_modulate",
    )(xc, g.reshape(1, -1), mod, mod)


def _matmul_kernel(a_ref, w_ref, o_ref):
    o_ref[...] = jnp.dot(a_ref[...], w_ref[...], preferred_element_type=F32).astype(o_ref.dtype)


def _in_proj(h, w):
    m, kdim = h.shape
    n = w.shape[1]
    tm = _largest_tile(m, 2048, 256)
    tn = 1152
    return pl.pallas_call(
        _matmul_kernel,
        grid=(m // tm, n // tn),
        in_specs=[pl.BlockSpec((tm, kdim), lambda i, j: (i, 0)),
                  pl.BlockSpec((kdim, tn), lambda i, j: (0, j))],
        out_specs=pl.BlockSpec((tm, tn), lambda i, j: (i, j)),
        out_shape=jax.ShapeDtypeStruct((m, n), F32),
        compiler_params=_cparams(2),
        name="in_proj",
    )(h, w)


def _chunk_order(j, nc_ctx, nc, rev):
    if not rev:
        return j
    return jnp.where(j < nc_ctx, nc_ctx - 1 - j, nc - 1 - (j - nc_ctx))


def _gla_kernel(q_ref, k_ref, v_ref, sm_ref, w2_ref, b2_ref, *rest, rev, final):
    if final:
        of_ref, r_ref, gn_ref, o_ref, st_ref, b_ref = rest
    else:
        o_ref, st_ref, b_ref = rest
    L, S = GLA_CHUNK, GLA_SUB
    rows = q_ref.shape[0]

    @pl.when(pl.program_id(1) == 0)
    def _():
        st_ref[...] = jnp.zeros_like(st_ref)

    logits = _dot3(sm_ref[...], w2_ref[0]) + b2_ref[0]
    g = (jnp.minimum(logits, 0.0) - jnp.log(1.0 + jnp.exp(-jnp.abs(logits)))) * (LOG2E / GLA_GATE_TEMP)
    rr, cc = _iota((rows, rows), 0), _iota((rows, rows), 1)
    tri = jnp.where((rr // L == cc // L) & ((cc >= rr) if rev else (cc <= rr)), 1.0, 0.0)
    b_ref[...] = _dot_exact_rhs(tri, g, 3)

    head_of_lane = _iota((1, GLA_QK), 1) // GLA_DK
    seg = jnp.where(_iota((GLA_QK, GLA_V), 0) // GLA_DK == _iota((GLA_QK, GLA_V), 1) // GLA_DV, 1.0, 0.0)
    sub_row = _iota((S, 1), 0)
    by_head = lambda x: [jnp.where(head_of_lane == h, x, 0.0) for h in range(GLA_HEADS)]
    n_sub = L // S
    order = list(reversed(range(rows // L))) if rev else list(range(rows // L))
    last = 0 if rev else L - 1

    groups = []
    for i in range(L // (2 * S)):
        r0 = 2 * S * i
        if (not rev) and i > 0:
            groups.append((r0, 2 * S, 0, r0, r0 - 1))
        if rev and r0 + 2 * S < L:
            groups.append((r0, 2 * S, r0 + 2 * S, L - r0 - 2 * S, r0 + 2 * S))
        groups.append((r0, S, r0 + S, S, r0 + S) if rev else (r0 + S, S, r0, S, r0 + S - 1))

    vals = {c: (q_ref[L * c:L * (c + 1)] * (GLA_DK ** -0.5), k_ref[L * c:L * (c + 1)],
                v_ref[L * c:L * (c + 1)], b_ref[L * c:L * (c + 1)]) for c in order}

    scores = {}
    for c in order:
        q, k, v, b = vals[c]
        for gi, (q0, nq, k0, nk, edge) in enumerate(groups):
            ref = b[edge:edge + 1]
            qd = q[q0:q0 + nq] * jnp.exp2(b[q0:q0 + nq] - ref)
            kd = k[k0:k0 + nk] * jnp.exp2(ref - b[k0:k0 + nk])
            scores[c, gi] = _dot(jnp.concatenate(by_head(qd), axis=0), kd, NT)

    ps = []
    for c in order:
        q, k, v, b = vals[c]
        for i in range(n_sub):
            qi, ki, bi = q[S * i:S * (i + 1)], k[S * i:S * (i + 1)], b[S * i:S * (i + 1)]
            for s in range(S):
                valid = (sub_row <= s) if rev else (sub_row >= s)
                ps.append(qi * jnp.exp2(jnp.where(valid, bi - bi[s:s + 1], NEG)) * ki[s:s + 1])
    w_all = _dot(jnp.concatenate(ps, axis=0), seg)

    intra = {}
    for ci, c in enumerate(order):
        q, k, v, b = vals[c]
        pieces = [None] * n_sub

        def add(i, val):
            pieces[i] = val if pieces[i] is None else pieces[i] + val

        for gi, (q0, nq, k0, nk, edge) in enumerate(groups):
            s_g = scores[c, gi]
            val = jnp.concatenate([_dot(s_g[nq * h:nq * (h + 1)], v[k0:k0 + nk, GLA_DV * h:GLA_DV * (h + 1)])
                                   for h in range(GLA_HEADS)], axis=1)
            for j in range(nq // S):
                add(q0 // S + j, val[S * j:S * (j + 1)])
        for i in range(n_sub):
            vi = v[S * i:S * (i + 1)]
            base = (ci * n_sub + i) * S * S
            for s in range(S):
                add(i, w_all[base + S * s:base + S * (s + 1)] * vi[s:s + 1])
        intra[c] = jnp.concatenate(pieces, axis=0)

    st = st_ref[...]
    for c in order:
        q, k, v, b = vals[c]
        b_last = b[last:last + 1]
        o = intra[c] + jnp.concatenate([_dot(qh, st, NT) for qh in by_head(q * jnp.exp2(b))], axis=1)
        kk = k * jnp.exp2(b_last - b)
        new = None
        for h in range(GLA_HEADS):
            u = _dot(v[:, GLA_DV * h:GLA_DV * (h + 1)], kk, TN)
            new = u if new is None else jnp.where(head_of_lane == h, u, new)
        st = jnp.exp2(b_last) * st + new
        if final:
            tot = o + of_ref[L * c:L * (c + 1)]
            outs = []
            for h in range(GLA_HEADS):
                oh = tot[:, GLA_DV * h:GLA_DV * (h + 1)]
                outs.append(oh * lax.rsqrt(jnp.mean(oh * oh, axis=-1, keepdims=True) + EPS) * gn_ref[...])
            o_ref[L * c:L * (c + 1)] = (jnp.concatenate(outs, axis=1) * _silu(r_ref[L * c:L * (c + 1)])).astype(o_ref.dtype)
        else:
            o_ref[L * c:L * (c + 1)] = o
    st_ref[...] = st


def _gla_pass(p, w2, b2, bsz, nt_rows, n_ctx, rev, extra=None):
    rows_per_step = GLA_STEP * GLA_CHUNK
    nb, nb_ctx = nt_rows // rows_per_step, n_ctx // rows_per_step

    def rows(width, col):
        return pl.BlockSpec((rows_per_step, width), lambda b, j: (b * nb + _chunk_order(j, nb_ctx, nb, rev), col))

    d = 1 if rev else 0
    in_specs = [rows(GLA_QK, P_OFF['gla_q'] // GLA_QK), rows(GLA_QK, P_OFF['gla_k'] // GLA_QK),
                rows(GLA_V, P_OFF['gla_v'] // GLA_V), rows(LANE, SMALL_OFF // LANE),
                pl.BlockSpec((1, LANE, GLA_QK), lambda b, j: (d, 0, 0)),
                pl.BlockSpec((1, 1, GLA_QK), lambda b, j: (d, 0, 0))]
    w2_lanes = jnp.zeros((2, LANE, GLA_QK), F32)
    w2_lanes = w2_lanes.at[0, SM_GF:SM_GF + GLA_GATE_RANK].set(w2[0]).at[1, SM_GB:SM_GB + GLA_GATE_RANK].set(w2[1])
    args = [p, p, p, p, w2_lanes, b2.reshape(2, 1, GLA_QK)]
    final = extra is not None
    if final:
        o_first, gn = extra
        in_specs += [rows(GLA_V, 0), rows(GLA_V, P_OFF['gla_r'] // GLA_V),
                     pl.BlockSpec((1, GLA_DV), lambda b, j: (0, 0))]
        args += [o_first, p, gn.reshape(1, GLA_DV)]
    return pl.pallas_call(
        functools.partial(_gla_kernel, rev=rev, final=final),
        grid=(bsz, nb),
        in_specs=in_specs,
        out_specs=rows(GLA_V, 0),
        out_shape=jax.ShapeDtypeStruct((bsz * nt_rows, GLA_V), BF16 if final else F32),
        scratch_shapes=[pltpu.VMEM((GLA_DV, GLA_QK), F32), pltpu.VMEM((rows_per_step, GLA_QK), F32)],
        compiler_params=_cparams(2),
        name="gla_bwd_out" if final else "gla_fwd",
    )(*args)


def _att_kernel(sink_ref, q_ref, kp_ref, kc_ref, kn_ref, vp_ref, vc_ref, vn_ref,
                cp_ref, sp_ref, cc_ref, sc_ref, cn_ref, sn_ref, kx_ref, vx_ref, qg_ref, kg_ref,
                o_ref, *, nb_ctx, nb_lat, n_ctx):
    j = pl.program_id(1)
    blk = ATT_BLOCK
    lane = _iota((1, LANE), 1)
    half_mean = jnp.where(_iota((LANE, LANE), 0) // ATT_HD == _iota((LANE, LANE), 1) // ATT_HD,
                          1.0 / ATT_HD, 0.0)

    def norm(x, g_ref):
        return x * lax.rsqrt(_dot(x * x, half_mean) + EPS) * g_ref[...]

    def rope(x, c_ref, s_ref):
        swapped = jnp.where(lane % 32 < 16, pltpu.roll(x, LANE - 16, 1), pltpu.roll(x, 16, 1))
        return x * c_ref[...] + swapped * s_ref[...]

    def attend(qpairs, keys, vals, bias):
        keys_sw = pltpu.roll(keys, ATT_HD, 1).astype(BF16)
        vals_sw = pltpu.roll(vals, ATT_HD, 1).astype(BF16)
        keys, vals = keys.astype(BF16), vals.astype(BF16)
        rep = ATT_Q_HEADS // ATT_KV_HEADS
        heads = range(ATT_Q_HEADS)
        straight = [(h % 2) == (h // rep) for h in heads]
        scores = []
        for h in heads:
            qm = jnp.where((lane >= ATT_HD) if h % 2 else (lane < ATT_HD), qpairs[h // 2], 0.0)
            s = _dot(qm, keys if straight[h] else keys_sw, NT)
            if bias is not None:
                for c0, add in bias:
                    parts = [s[:, :c0], s[:, c0:c0 + blk] + add, s[:, c0 + blk:]]
                    s = jnp.concatenate([part for part in parts if part.shape[1]], axis=1)
            scores.append(s)
        probs, dens = [], []
        for h in heads:
            sk = sink_ref[0, h] * LOG2E
            m = jnp.maximum(jnp.max(scores[h], axis=-1, keepdims=True), sk)
            pr = jnp.exp2(scores[h] - m)
            probs.append(pr)
            dens.append(jnp.sum(pr, axis=-1, keepdims=True) + jnp.exp2(sk - m))
        outs = [_dot(probs[h], vals if straight[h] else vals_sw) / dens[h] for h in heads]
        o_ref[...] = jnp.concatenate([jnp.where(lane < ATT_HD, outs[2 * pi], outs[2 * pi + 1])
                                      for pi in range(ATT_Q_HEADS // 2)], axis=1).astype(o_ref.dtype)

    kx = norm(kx_ref[...], kg_ref)
    vx = vx_ref[...]
    q_scale = ATT_HD ** -0.5 * LOG2E

    @pl.when(j < nb_ctx)
    def _():
        qpairs = [norm(q_ref[:, LANE * pi:LANE * (pi + 1)], qg_ref) * q_scale
                  for pi in range(ATT_Q_HEADS // 2)]
        attend(qpairs, kx, vx, None)

    @pl.when(j >= nb_ctx)
    def _():
        li = j - nb_ctx
        qpairs = [rope(norm(q_ref[:, LANE * pi:LANE * (pi + 1)], qg_ref), cc_ref, sc_ref) * q_scale
                  for pi in range(ATT_Q_HEADS // 2)]
        keys = jnp.concatenate([kx, rope(norm(kp_ref[...], kg_ref), cp_ref, sp_ref),
                                rope(norm(kc_ref[...], kg_ref), cc_ref, sc_ref),
                                rope(norm(kn_ref[...], kg_ref), cn_ref, sn_ref)], axis=0)
        vals = jnp.concatenate([vx, vp_ref[...], vc_ref[...], vn_ref[...]], axis=0)
        a, jj = _iota((blk, blk), 0), _iota((blk, blk), 1)
        prev_ok = (jj >= a) & (li > 0)
        next_ok = (jj <= a) & (li < nb_lat - 1)
        bias = [(n_ctx, jnp.where(prev_ok, 0.0, NEG)), (n_ctx + 2 * blk, jnp.where(next_ok, 0.0, NEG))]
        attend(qpairs, keys, vals, bias)


def _rope_tables(n):
    rows = n // GRID_W
    row = jnp.repeat(jnp.arange(rows), GRID_W).astype(F32)
    col = jnp.tile(jnp.arange(GRID_W), rows).astype(F32)
    axis_dim = ATT_HD // 2
    inv_freq = ROPE_BASE ** (-jnp.arange(0, axis_dim, 2, dtype=F32) / axis_dim)
    ang_r, ang_c = row[:, None] * inv_freq, col[:, None] * inv_freq
    cos = jnp.concatenate([jnp.cos(ang_r), jnp.cos(ang_r), jnp.cos(ang_c), jnp.cos(ang_c)], axis=1)
    sin = jnp.concatenate([-jnp.sin(ang_r), jnp.sin(ang_r), -jnp.sin(ang_c), jnp.sin(ang_c)], axis=1)
    return jnp.tile(cos, (1, LANE // ATT_HD)), jnp.tile(sin, (1, LANE // ATT_HD))


def _attention(p, cos, sin, sink, qg, kg, bsz, nt_rows, n_ctx):
    blk = ATT_BLOCK
    nb, nb_ctx = nt_rows // blk, n_ctx // blk
    nb_lat = nb - nb_ctx
    k_col, v_col = P_OFF['att_k'] // LANE, P_OFF['att_v'] // LANE

    def lat(j, d):
        return jnp.clip(j - nb_ctx + d, 0, nb_lat - 1)

    def kv(col, d):
        return pl.BlockSpec((blk, LANE), lambda b, j: (b * nb + nb_ctx + lat(j, d), col))

    def tab(d):
        return pl.BlockSpec((blk, LANE), lambda b, j: (lat(j, d), 0))

    ctx_rows = nt_rows // n_ctx
    in_specs = [pl.BlockSpec(memory_space=pltpu.SMEM),
                pl.BlockSpec((blk, ATT_Q), lambda b, j: (b * nb + j, P_OFF['att_q'] // ATT_Q)),
                kv(k_col, -1), kv(k_col, 0), kv(k_col, 1), kv(v_col, -1), kv(v_col, 0), kv(v_col, 1),
                tab(-1), tab(-1), tab(0), tab(0), tab(1), tab(1),
                pl.BlockSpec((n_ctx, LANE), lambda b, j: (b * ctx_rows, k_col)),
                pl.BlockSpec((n_ctx, LANE), lambda b, j: (b * ctx_rows, v_col)),
                pl.BlockSpec((1, LANE), lambda b, j: (0, 0)),
                pl.BlockSpec((1, LANE), lambda b, j: (0, 0))]
    tile2 = lambda g: jnp.tile(g.reshape(1, ATT_HD), (1, LANE // ATT_HD))
    return pl.pallas_call(
        functools.partial(_att_kernel, nb_ctx=nb_ctx, nb_lat=nb_lat, n_ctx=n_ctx),
        grid=(bsz, nb),
        in_specs=in_specs,
        out_specs=pl.BlockSpec((blk, ATT_Q), lambda b, j: (b * nb + j, 0)),
        out_shape=jax.ShapeDtypeStruct((bsz * nt_rows, ATT_Q), BF16),
        compiler_params=_cparams(2),
        name="attention",
    )(sink.reshape(1, ATT_Q_HEADS), p, p, p, p, p, p, p, cos, sin, cos, sin, cos, sin, p, p, tile2(qg), tile2(kg))


def _conv_kernel(xp_ref, xc_ref, xn_ref, w_ref, b_ref, o_ref, ext_ref, *, nct, nt):
    j = pl.program_id(1)
    rows = xc_ref.shape[0]
    no_prev = (j == 0) | (j == nct)
    no_next = (j == nct - 1) | (j == nt - 1)
    ext_ref[0:8] = jnp.where(no_prev, 0.0, xp_ref[...])
    ext_ref[8:8 + rows] = xc_ref[...]
    ext_ref[8 + rows:16 + rows] = jnp.where(no_next, 0.0, xn_ref[...])
    pad = SSD_CONV // 2
    ext = ext_ref[...]
    acc = b_ref[...] + w_ref[pad:pad + 1] * ext[8:8 + rows]
    for t in range(SSD_CONV):
        if t != pad:
            acc = acc + w_ref[t:t + 1] * pltpu.roll(ext, (pad - t) % ext.shape[0], 0)[8:8 + rows]
    o_ref[...] = _silu(acc)


def _ssd_conv(p, w, bias, bsz, nt_rows, n_ctx):
    tiles, nct = nt_rows // ROW_TILE, n_ctx // ROW_TILE
    ct = 512
    c0 = P_OFF['ssd_xbc'] // ct
    r8 = ROW_TILE // 8
    last8 = bsz * nt_rows // 8 - 1
    return pl.pallas_call(
        functools.partial(_conv_kernel, nct=nct, nt=tiles),
        grid=(bsz, tiles, SSD_XBC // ct),
        in_specs=[pl.BlockSpec((8, ct), lambda b, j, c: (jnp.maximum((b * tiles + j) * r8 - 1, 0), c0 + c)),
                  pl.BlockSpec((ROW_TILE, ct), lambda b, j, c: (b * tiles + j, c0 + c)),
                  pl.BlockSpec((8, ct), lambda b, j, c: (jnp.minimum((b * tiles + j + 1) * r8, last8), c0 + c)),
                  pl.BlockSpec((SSD_CONV, ct), lambda b, j, c: (0, c)),
                  pl.BlockSpec((1, ct), lambda b, j, c: (0, c))],
        out_specs=pl.BlockSpec((ROW_TILE, ct), lambda b, j, c: (b * tiles + j, c)),
        out_shape=jax.ShapeDtypeStruct((bsz * nt_rows, SSD_XBC), F32),
        scratch_shapes=[pltpu.VMEM((ROW_TILE + 16, ct), F32)],
        compiler_params=_cparams(3),
        name="ssd_conv",
    )(p, p, p, w, bias.reshape(1, -1))


def _ssd_kernel(x_ref, bc_ref, sm_ref, dtb_ref, dtbc_ref, al_ref, alc_ref, *rest, rev, final):
    if final:
        yf_ref, z_ref, d_ref, gn_ref, o_ref, st_ref = rest
    else:
        o_ref, st_ref = rest
    L, P, N = SSD_CHUNK, SSD_HD, SSD_STATE
    gw = SSD_HPG * P

    @pl.when(pl.program_id(1) == 0)
    def _():
        st_ref[...] = jnp.zeros_like(st_ref)

    xs = x_ref[...]
    bm, cm = bc_ref[:, 0:SSD_BC], bc_ref[:, SSD_BC:2 * SSD_BC]
    off = SM_DTB if rev else SM_DTF
    sm = sm_ref[...]
    dt = _softplus(sm + dtb_ref[0])
    a = -dt * jnp.exp(al_ref[0])
    dt_t = _softplus(sm.T + dtbc_ref[0])
    a_t = -dt_t * jnp.exp(alc_ref[0])
    tt, ss = _iota((L, L), 0), _iota((L, L), 1)
    causal = (ss >= tt) if rev else (ss <= tt)
    tri = jnp.where(causal, 1.0, 0.0)
    cum = _dot_exact_rhs(tri, a, 2)
    cum_t = _dot_exact_lhs(a_t, tri, 2, NT)
    last = 0 if rev else L - 1
    cum_last = cum[last:last + 1]
    expand = jnp.where(_iota((LANE, SSD_INNER), 0) - off == _iota((LANE, SSD_INNER), 1) // P, 1.0, 0.0)
    ecum = _dot(jnp.exp(cum), expand)
    wgt = _dot(jnp.exp(cum_last - cum) * dt, expand)

    st = st_ref[...]
    lane = _iota((1, LANE), 1)
    ys = []
    for g in range(SSD_GROUPS):
        cg, bg = cm[:, N * g:N * (g + 1)], bm[:, N * g:N * (g + 1)]
        cb = _dot(cg, bg, NT)
        for jp in range(SSD_HPG // 2):
            h0 = SSD_HPG * g + 2 * jp
            ms = []
            for h in (off + h0, off + h0 + 1):
                dec = jnp.exp(jnp.where(causal, cum[:, h:h + 1] - cum_t[h:h + 1, :], NEG))
                ms.append(cb * dec * dt_t[h:h + 1, :])
            xp = xs[:, P * h0:P * h0 + LANE]
            rhs = jnp.concatenate([jnp.where(lane < P, xp, 0.0), jnp.where(lane >= P, xp, 0.0)], axis=0)
            ys.append(_dot(jnp.concatenate(ms, axis=1), rhs))
    y = jnp.concatenate(ys, axis=1)
    y = y + jnp.concatenate([_dot(cm[:, N * g:N * (g + 1)], st[:, gw * g:gw * (g + 1)])
                             for g in range(SSD_GROUPS)], axis=1) * ecum
    xw = xs * wgt
    new = jnp.concatenate([_dot(bm[:, N * g:N * (g + 1)], xw[:, gw * g:gw * (g + 1)], TN)
                           for g in range(SSD_GROUPS)], axis=1)
    st_ref[...] = ecum[last:last + 1] * st + new

    if final:
        tot = (y + yf_ref[...] + d_ref[...] * xs) * _silu(z_ref[...])
        outs = []
        for g in range(SSD_GROUPS):
            tg = tot[:, gw * g:gw * (g + 1)]
            outs.append(tg * lax.rsqrt(jnp.mean(tg * tg, axis=-1, keepdims=True) + EPS))
        o_ref[...] = (jnp.concatenate(outs, axis=1) * gn_ref[...]).astype(o_ref.dtype)
    else:
        o_ref[...] = y


def _ssd_pass(p, xbc, dt_bias, a_log, bsz, nt_rows, n_ctx, rev, extra=None):
    L = SSD_CHUNK
    nc, nc_ctx = nt_rows // L, n_ctx // L

    def rows(width, col):
        return pl.BlockSpec((L, width), lambda b, j: (b * nc + _chunk_order(j, nc_ctx, nc, rev), col))

    d = 1 if rev else 0
    vec = pl.BlockSpec((1, 1, LANE), lambda b, j: (d, 0, 0))
    colv = pl.BlockSpec((1, LANE, 1), lambda b, j: (d, 0, 0))
    in_specs = [rows(SSD_INNER, 0), rows(2 * SSD_BC, SSD_INNER // (2 * SSD_BC)), rows(LANE, SMALL_OFF // LANE),
                vec, colv, vec, colv]

    def lanes(v):
        out = jnp.zeros((2, LANE), F32)
        return out.at[0, SM_DTF:SM_DTF + SSD_HEADS].set(v[0]).at[1, SM_DTB:SM_DTB + SSD_HEADS].set(v[1])

    dtb, alog = lanes(dt_bias), lanes(a_log)
    args = [xbc, xbc, p, dtb.reshape(2, 1, LANE), dtb.reshape(2, LANE, 1),
            alog.reshape(2, 1, LANE), alog.reshape(2, LANE, 1)]
    final = extra is not None
    if final:
        y_first, d_skip, gn = extra
        full = pl.BlockSpec((1, SSD_INNER), lambda b, j: (0, 0))
        in_specs += [rows(SSD_INNER, 0), rows(SSD_INNER, P_OFF['ssd_z'] // SSD_INNER), full, full]
        args += [y_first, p, jnp.repeat(d_skip, SSD_HD).reshape(1, SSD_INNER), gn.reshape(1, SSD_INNER)]
    return pl.pallas_call(
        functools.partial(_ssd_kernel, rev=rev, final=final),
        grid=(bsz, nc),
        in_specs=in_specs,
        out_specs=rows(SSD_INNER, 0),
        out_shape=jax.ShapeDtypeStruct((bsz * nt_rows, SSD_INNER), BF16 if final else F32),
        scratch_shapes=[pltpu.VMEM((SSD_STATE, SSD_INNER), F32)],
        compiler_params=_cparams(2),
        name="ssd_bwd_out" if final else "ssd_fwd",
    )(*args)


def _route(h2, rw_ref, rb_ref, run_ref):
    rows = h2.shape[0]
    scores = _sigmoid(_dot3(h2, rw_ref[...]).T[0:N_EXPERTS])
    biased = scores + rb_ref[...]
    row = lambda a, e: a[e:e + 1, :]
    epg = EXPERTS_PER_GROUP
    best = gidx = None
    for g in range(N_EXPERT_GROUPS):
        vals = [row(biased, epg * g + i) for i in range(epg)]
        top2 = None
        for i in range(epg):
            for i2 in range(i + 1, epg):
                pair = vals[i] + vals[i2]
                top2 = pair if top2 is None else jnp.maximum(top2, pair)
        if g == 0:
            best, gidx = top2, jnp.zeros(top2.shape, jnp.int32)
        else:
            take = top2 > best
            best, gidx = jnp.where(take, top2, best), jnp.where(take, g, gidx)

    def pick(a, i):
        out = row(a, i)
        for g in range(1, N_EXPERT_GROUPS):
            out = jnp.where(gidx == g, row(a, epg * g + i), out)
        return out

    cand = [pick(biased, i) for i in range(epg)]
    raw = [pick(scores, i) for i in range(epg)]

    def argmax_first(vals, exclude):
        bv = bi = bs = None
        for i in range(epg):
            v = vals[i] if exclude is None else jnp.where(exclude == i, -jnp.inf, vals[i])
            if bv is None:
                bv, bi, bs = v, jnp.zeros(v.shape, jnp.int32), raw[0]
            else:
                take = v > bv
                bv, bi, bs = jnp.where(take, v, bv), jnp.where(take, i, bi), jnp.where(take, raw[i], bs)
        return bi, bs

    i0, s0 = argmax_first(cand, None)
    i1, s1 = argmax_first(cand, i0)
    e0, e1 = gidx * epg + i0, gidx * epg + i1
    tot = s0 + s1

    eid = _iota((N_EXPERTS, rows), 0)
    oh0 = jnp.where(eid == e0, 1.0, 0.0)
    oh1 = jnp.where(eid == e1, 1.0, 0.0)
    cnt = oh0 + oh1
    before = jnp.where(_iota((rows, rows), 0) < _iota((rows, rows), 1), 1.0, 0.0)
    prefix = _dot(cnt, before) + run_ref[:, 0:1]
    r0 = jnp.sum(oh0 * prefix, axis=0, keepdims=True).astype(jnp.int32)
    r1 = jnp.sum(oh1 * prefix, axis=0, keepdims=True).astype(jnp.int32)
    run_ref[...] = run_ref[...] + jnp.sum(cnt, axis=1, keepdims=True)
    return (e0, e1), (s0 / tot, s1 / tot), (r0, r1)


def _merge_kernel(yg_ref, ya_ref, ys_ref, mg_ref, x_ref, gb_ref, g1_ref, sh2_ref, sc2_ref, n2_ref,
                  wa_ref, wb_ref, wc_ref, wo_ref, rw_ref, rb_ref,
                  xo_ref, hx_ref, e_ref, g_ref, r_ref, c_ref, run_ref):
    d = D_MODEL
    rows = x_ref.shape[0]
    tile = pl.program_id(0) * pl.num_programs(1) + pl.program_id(1)

    @pl.when(tile == 0)
    def _():
        run_ref[...] = jnp.zeros_like(run_ref)

    gates = _sigmoid(mg_ref[...] + gb_ref[...])
    merged = (gates[:, 0:d] * _dot(yg_ref[...], wa_ref[...])
              + gates[:, d:2 * d] * _dot(ya_ref[...], wb_ref[...])
              + gates[:, 2 * d:3 * d] * _dot(ys_ref[...], wc_ref[...]))
    xn = x_ref[...] + g1_ref[0] * _dot(merged, wo_ref[...])
    xo_ref[...] = xn
    h2 = _normmod(xn, n2_ref[...], sh2_ref[0], sc2_ref[0])
    (e0, e1), (g0, g1), (r0, r1) = _route(h2, rw_ref, rb_ref, run_ref)
    e_ref[...] = jnp.concatenate([e0, e1], axis=0)
    g_ref[...] = jnp.concatenate([g0, g1], axis=0)
    r_ref[...] = jnp.concatenate([r0, r1], axis=0)
    c_ref[...] = run_ref[...].astype(jnp.int32)
    token1 = (tile * rows + _iota((rows, 1), 0) + 1).astype(F32)
    first_expert = jnp.broadcast_to(e0.astype(F32), (ID_LANES, rows)).T
    hx_ref[:, 0:d] = h2
    hx_ref[:, d:d + ID_LANES] = jnp.where(_iota((1, ID_LANES), 1) == 0, token1, first_expert)


def _merge(y_gla, y_att, y_ssd, p, xc, gate_b, mod, norm2_g, wa, wb, wc, wo, router_w, router_b, bsz, nt_rows, nct):
    tiles = nt_rows // ROW_TILE
    d = D_MODEL
    t = xc.shape[0]

    def rows(width, col=0):
        return pl.BlockSpec((ROW_TILE, width), lambda b, j: (b * tiles + j, col))

    def whole(a):
        return pl.BlockSpec(a.shape, lambda b, j: (0,) * a.ndim)

    gb = gate_b.reshape(1, N_BRANCH * d)
    n2 = norm2_g.reshape(1, d)
    rw = jnp.pad(router_w, ((0, 0), (0, LANE - N_EXPERTS)))
    rb = router_b.reshape(N_EXPERTS, 1)
    lanes = pl.BlockSpec((2, ROW_TILE), lambda b, j: (0, b * tiles + j))
    return pl.pallas_call(
        _merge_kernel,
        grid=(bsz, tiles),
        in_specs=[rows(GLA_V), rows(ATT_Q), rows(SSD_INNER), rows(N_BRANCH * d, P_OFF['merge'] // (N_BRANCH * d)),
                  rows(d), whole(gb), _mod_spec(2, nct), _mod_spec(3, nct), _mod_spec(4, nct), whole(n2),
                  whole(wa), whole(wb), whole(wc), whole(wo), whole(rw), whole(rb)],
        out_specs=[rows(d), rows(H_EXT), lanes, lanes, lanes, pl.BlockSpec((N_EXPERTS, LANE), lambda b, j: (0, 0))],
        out_shape=[jax.ShapeDtypeStruct((t, d), F32), jax.ShapeDtypeStruct((t, H_EXT), F32),
                   jax.ShapeDtypeStruct((2, t), jnp.int32), jax.ShapeDtypeStruct((2, t), F32),
                   jax.ShapeDtypeStruct((2, t), jnp.int32), jax.ShapeDtypeStruct((N_EXPERTS, LANE), jnp.int32)],
        scratch_shapes=[pltpu.VMEM((N_EXPERTS, LANE), F32)],
        compiler_params=_cparams(2),
        name="merge",
    )(y_gla, y_att, y_ssd, p, xc, gb, mod, mod, mod, n2, wa, wb, wc, wo, rw, rb)


def _dispatch_kernel(pe_ref, dest_ref, h_ref, buf_ref, zero_ref, sem, *, n_slots):
    rows = h_ref.shape[0]

    @pl.when(pl.program_id(0) == 0)
    def _():
        zero_ref[...] = jnp.zeros_like(zero_ref)

        def zero_block(first):
            def one(r, carry):
                pltpu.make_async_copy(zero_ref.at[pl.ds(0, 1)], buf_ref.at[pl.ds(first + r, 1)], sem).start()
                return carry
            lax.fori_loop(0, rows, one, 0)
            pltpu.make_async_copy(h_ref, buf_ref.at[pl.ds(0, rows)], sem).wait()

        def per_expert(e, carry):
            zero_block(jnp.maximum(pe_ref[e] - rows, 0))
            zero_block(n_slots - (e + 1) * rows)
            return carry

        lax.fori_loop(0, N_EXPERTS, per_expert, 0)

    def row_copy(r, k):
        return pltpu.make_async_copy(h_ref.at[pl.ds(r, 1)], buf_ref.at[pl.ds(dest_ref[k, r], 1)], sem)

    def start(r, carry):
        row_copy(r, 0).start()
        row_copy(r, 1).start()
        return carry

    lax.fori_loop(0, rows, start, 0, unroll=DMA_UNROLL)
    for _ in range(2):
        pltpu.make_async_copy(h_ref, buf_ref.at[pl.ds(0, rows)], sem).wait()


def _dispatch(hx, dest, pad_end, n_slots):
    t, w = hx.shape
    assert ROW_TILE == MOE_BLOCK
    grid_spec = pltpu.PrefetchScalarGridSpec(
        num_scalar_prefetch=1,
        grid=(t // ROW_TILE,),
        in_specs=[pl.BlockSpec((2, ROW_TILE), lambda i, pe: (0, i), memory_space=pltpu.SMEM),
                  pl.BlockSpec((ROW_TILE, w), lambda i, pe: (i, 0))],
        out_specs=pl.BlockSpec(memory_space=pl.ANY),
        scratch_shapes=[pltpu.VMEM((8, w), F32), pltpu.SemaphoreType.DMA(())],
    )
    return pl.pallas_call(
        functools.partial(_dispatch_kernel, n_slots=n_slots),
        grid_spec=grid_spec,
        out_shape=jax.ShapeDtypeStruct((n_slots, w), F32),
        compiler_params=_cparams(1),
        name="moe_dispatch",
    )(pad_end.astype(jnp.int32), dest, hx)


def _expert_kernel(be_ref, real_ref, x_ref, w1_ref, w3_ref, w2_ref, o2_ref, obuf, tvec, tsm, sem_t, sem_o, fence, *,
                   n_rows2, n_slots):
    i = pl.program_id(0)
    rows = x_ref.shape[0]
    cur = i % 2
    prv = 1 - cur
    d = D_MODEL

    def send(slot, r):
        return pltpu.make_async_copy(obuf.at[slot, pl.ds(r, 1)], o2_ref.at[pl.ds(tsm[slot, 0, r], 1)], sem_o.at[slot])

    def send_row(slot, r, carry):
        send(slot, r).start()
        return carry

    def drain(slot):
        pltpu.make_async_copy(obuf.at[slot], o2_ref.at[pl.ds(0, rows)], sem_o.at[slot]).wait()

    @pl.when(i == 0)
    def _():
        obuf[...] = jnp.zeros_like(obuf)

        def spare(r, carry):
            tsm[0, 0, r] = n_slots + r
            tsm[1, 0, r] = n_slots + rows + r
            return carry

        lax.fori_loop(0, rows, spare, 0)
        lax.fori_loop(0, rows, functools.partial(send_row, 0), 0)

    drain(cur)

    origin = x_ref[:, d:d + ID_LANES].T
    token1 = origin[0:1].astype(jnp.int32)
    plane = jnp.where(origin[1:2].astype(jnp.int32) == be_ref[i], 0, n_rows2 // 2)
    empty_row = n_rows2 + (i * rows - real_ref[i]) + _iota((1, rows), 1)
    tvec[...] = jnp.broadcast_to(jnp.where(token1 > 0, plane + token1 - 1, empty_row), tvec.shape)
    to_smem = pltpu.make_async_copy(tvec, tsm.at[cur], sem_t)
    to_smem.start()

    early = (2 * rows) // 3
    for r in range(early):
        send(prv, r).start()
    x = x_ref[:, 0:d].astype(BF16)
    hid = _silu(_dot(x, w1_ref[0])) * _dot(x, w3_ref[0])
    pl.semaphore_signal(fence, 1)
    pl.semaphore_wait(fence, 1)
    for r in range(early, rows):
        send(prv, r).start()
    obuf[cur] = _dot(hid, w2_ref[0])
    to_smem.wait()

    @pl.when(i == pl.num_programs(0) - 1)
    def _():
        lax.fori_loop(0, rows, functools.partial(send_row, cur), 0)
        drain(prv)
        drain(cur)


def _experts(buf, block_e, block_real, w1, w3, w2, n_rows2):
    n_slots, w = buf.shape
    d, ff = w1.shape[1], w1.shape[2]
    grid_spec = pltpu.PrefetchScalarGridSpec(
        num_scalar_prefetch=2,
        grid=(n_slots // MOE_BLOCK,),
        in_specs=[pl.BlockSpec((MOE_BLOCK, w), lambda i, be, br: (i, 0)),
                  pl.BlockSpec((1, d, ff), lambda i, be, br: (be[i], 0, 0)),
                  pl.BlockSpec((1, d, ff), lambda i, be, br: (be[i], 0, 0)),
                  pl.BlockSpec((1, ff, d), lambda i, be, br: (be[i], 0, 0))],
        out_specs=pl.BlockSpec(memory_space=pl.ANY),
        scratch_shapes=[pltpu.VMEM((2, MOE_BLOCK, d), F32), pltpu.VMEM((8, MOE_BLOCK), jnp.int32),
                        pltpu.SMEM((2, 8, MOE_BLOCK), jnp.int32), pltpu.SemaphoreType.DMA(()),
                        pltpu.SemaphoreType.DMA((2,)), pltpu.SemaphoreType.REGULAR(())],
    )
    return pl.pallas_call(
        functools.partial(_expert_kernel, n_rows2=n_rows2, n_slots=n_slots),
        grid_spec=grid_spec,
        out_shape=jax.ShapeDtypeStruct((n_slots + 2 * MOE_BLOCK, d), F32),
        compiler_params=_cparams(1),
        name="moe_experts",
    )(block_e, block_real, buf, w1, w3, w2)


def _combine_kernel(oa_ref, ob_ref, g_ref, x_ref, g2_ref, *rest, with_next):
    y = g_ref[:, 0:1] * oa_ref[...] + g_ref[:, 1:2] * ob_ref[...]
    xo = x_ref[...] + g2_ref[0] * y
    if with_next:
        n1_ref, sh_ref, sc_ref, xo_ref, h_ref = rest
        h_ref[...] = _normmod(xo, n1_ref[...], sh_ref[0], sc_ref[0]).astype(h_ref.dtype)
    else:
        xo_ref, = rest
    xo_ref[...] = xo


def _combine(o2, gates_col, x_new, mod, bsz, nt_rows, nct, latent_only, nxt=None):
    tiles = nt_rows // ROW_TILE
    d = D_MODEL
    skip = nct if latent_only else 0
    out_tiles = tiles - skip
    src = lambda b, j: b * tiles + skip + j
    kind = lambda j: jnp.where(skip + j >= nct, 1, 0)
    second = bsz * tiles
    in_specs = [pl.BlockSpec((ROW_TILE, d), lambda b, j: (src(b, j), 0)),
                pl.BlockSpec((ROW_TILE, d), lambda b, j: (second + src(b, j), 0)),
                pl.BlockSpec((ROW_TILE, 2), lambda b, j: (src(b, j), 0)),
                pl.BlockSpec((ROW_TILE, d), lambda b, j: (src(b, j), 0)),
                pl.BlockSpec((1, 1, d), lambda b, j: (2 * b + kind(j), 0, 5))]
    args = [o2, o2, gates_col, x_new, mod]
    out_rows = pl.BlockSpec((ROW_TILE, d), lambda b, j: (b * out_tiles + j, 0))
    out_specs, out_shape = [out_rows], [jax.ShapeDtypeStruct((bsz * out_tiles * ROW_TILE, d), F32)]
    if nxt is not None:
        norm_g, mod_next = nxt
        in_specs += [pl.BlockSpec((1, d), lambda b, j: (0, 0)),
                     pl.BlockSpec((1, 1, d), lambda b, j: (2 * b + kind(j), 0, 0)),
                     pl.BlockSpec((1, 1, d), lambda b, j: (2 * b + kind(j), 0, 1))]
        args += [norm_g.reshape(1, d), mod_next, mod_next]
        out_specs.append(out_rows)
        out_shape.append(jax.ShapeDtypeStruct((bsz * out_tiles * ROW_TILE, d), BF16))
    return pl.pallas_call(
        functools.partial(_combine_kernel, with_next=nxt is not None),
        grid=(bsz, out_tiles),
        in_specs=in_specs,
        out_specs=out_specs,
        out_shape=out_shape,
        compiler_params=_cparams(2),
        name="moe_combine",
    )(*args)


def _moe(hx, routing, x_new, mod, w1, w3, w2, bsz, nt_rows, nct, latent_only, nxt):
    t = hx.shape[0]
    e_idx, gates, rank, counts = routing
    counts = counts[:, 0]
    padded = (counts + MOE_BLOCK - 1) // MOE_BLOCK * MOE_BLOCK
    pad_end = jnp.cumsum(padded)
    pad_start = pad_end - padded
    first = jnp.sum(jnp.where(e_idx[None] == jnp.arange(N_EXPERTS)[:, None, None], pad_start[:, None, None], 0),
                    axis=0)
    dest = first + rank
    n_blocks = -(-2 * t // MOE_BLOCK) + N_EXPERTS
    block_e = jnp.minimum(jnp.searchsorted(pad_end, jnp.arange(n_blocks, dtype=jnp.int32) * MOE_BLOCK,
                                           side='right'), N_EXPERTS - 1).astype(jnp.int32)
    block_real = jnp.sum(jnp.where(jnp.arange(N_EXPERTS)[None] <= block_e[:, None], counts[None], 0), axis=1)
    assert (2 * t) % MOE_BLOCK == 0
    buf = _dispatch(hx, dest, pad_end, n_blocks * MOE_BLOCK)
    o2 = _experts(buf, block_e, block_real.astype(jnp.int32), w1, w3, w2, 2 * t)
    return _combine(o2, gates.T, x_new, mod, bsz, nt_rows, nct, latent_only, nxt)


def _permute_w_in(w_in):
    offsets = np.concatenate([[0], np.cumsum(IN_SPLITS)])
    start = dict(zip(IN_NAMES, offsets[:-1]))
    width = dict(zip(IN_NAMES, IN_SPLITS))
    cols = [w_in[:, start[n]:start[n] + width[n]] for n in P_ORDER]
    cols.append(jnp.zeros((w_in.shape[0], SMALL_PAD), w_in.dtype))
    return jnp.concatenate(cols, axis=1).astype(BF16)


def _layer(xc, h, mod, lp, router_w, router_b, rope, bsz, nt_rows, n_ctx, nxt):
    nct = n_ctx // ROW_TILE
    if h is None:
        h = _norm_modulate(xc, lp['norm1_g'], mod, bsz, nt_rows, nct)
    p = _in_proj(h, _permute_w_in(lp['w_in']))

    o_f = _gla_pass(p, lp['gla_w2'], lp['gla_b2'], bsz, nt_rows, n_ctx, rev=False)
    y_gla = _gla_pass(p, lp['gla_w2'], lp['gla_b2'], bsz, nt_rows, n_ctx, rev=True,
                      extra=(o_f, lp['gla_norm_g']))
    y_att = _attention(p, rope[0], rope[1], lp['att_sink'], lp['q_norm_g'], lp['k_norm_g'], bsz, nt_rows, n_ctx)
    xbc = _ssd_conv(p, lp['ssd_conv_w'], lp['ssd_conv_b'], bsz, nt_rows, n_ctx)
    y_f = _ssd_pass(p, xbc, lp['ssd_dt_bias'], lp['ssd_a_log'], bsz, nt_rows, n_ctx, rev=False)
    y_ssd = _ssd_pass(p, xbc, lp['ssd_dt_bias'], lp['ssd_a_log'], bsz, nt_rows, n_ctx, rev=True,
                      extra=(y_f, lp['ssd_d'], lp['ssd_norm_g']))

    x_new, hx, *routing = _merge(y_gla, y_att, y_ssd, p, xc, lp['gate_b'], mod, lp['norm2_g'],
                                 lp['w_br_a'].astype(BF16), lp['w_br_b'].astype(BF16), lp['w_br_c'].astype(BF16),
                                 lp['w_out'].astype(BF16), router_w, router_b, bsz, nt_rows, nct)
    return _moe(hx, routing, x_new, mod, lp['w1'].astype(BF16), lp['w3'].astype(BF16), lp['w2'].astype(BF16),
                bsz, nt_rows, nct, latent_only=nxt is None, nxt=nxt)


def kernel(x, c, ctx, c_ctx, w_mod, b_mod, norm1_g, w_in, gla_w2, gla_b2, gla_norm_g, q_norm_g, k_norm_g, att_sink, ssd_conv_w, ssd_conv_b, ssd_dt_bias, ssd_a_log, ssd_d, ssd_norm_g, gate_b, w_br_a, w_br_b, w_br_c, w_out, norm2_g, router_w, router_b, w1, w3, w2):
    bsz, seq, d = x.shape
    n_ctx = ctx.shape[1]
    nt_rows = n_ctx + seq
    assert d == D_MODEL and n_ctx % ROW_TILE == 0 and seq % ROW_TILE == 0 and nt_rows % n_ctx == 0
    assert seq % GRID_W == 0 and bsz < 16

    cc = jnp.zeros((16, d), F32).at[:bsz].set(c).at[bsz].set(c_ctx)
    mods = _mod_vectors(cc, w_mod, b_mod)
    rope = _rope_tables(seq)
    xc = jnp.concatenate([ctx, x], axis=1).reshape(bsz * nt_rows, d)
    params = dict(norm1_g=norm1_g, w_in=w_in, gla_w2=gla_w2, gla_b2=gla_b2, gla_norm_g=gla_norm_g,
                  q_norm_g=q_norm_g, k_norm_g=k_norm_g, att_sink=att_sink, ssd_conv_w=ssd_conv_w,
                  ssd_conv_b=ssd_conv_b, ssd_dt_bias=ssd_dt_bias, ssd_a_log=ssd_a_log, ssd_d=ssd_d,
                  ssd_norm_g=ssd_norm_g, gate_b=gate_b, w_br_a=w_br_a, w_br_b=w_br_b, w_br_c=w_br_c,
                  w_out=w_out, norm2_g=norm2_g, w1=w1, w3=w3, w2=w2)

    def mod_rows(l):
        m = mods[l]
        return jnp.stack([jnp.broadcast_to(m[bsz], (bsz, N_MOD * d)), m[:bsz]], axis=1).reshape(2 * bsz, 1, N_MOD * d)

    h = None
    for l in range(DEPTH):
        lp = {name: val[l] for name, val in params.items()}
        nxt = (norm1_g[l + 1], mod_rows(l + 1)) if l + 1 < DEPTH else None
        out = _layer(xc, h, mod_rows(l), lp, router_w, router_b, rope, bsz, nt_rows, n_ctx, nxt)
        xc, h = (out[0], out[1]) if nxt is not None else (out[0], None)
    return xc.reshape(bsz, seq, d)
```

```python
import functools

import numpy as np
import jax
import jax.numpy as jnp
from jax import lax
from jax.experimental import pallas as pl
from jax.experimental.pallas import tpu as pltpu

F32 = jnp.float32
BF16 = jnp.bfloat16

D_MODEL = 1024
DEPTH = 2
GRID_W = 64
EPS = 1e-6
N_MOD = 6

GLA_HEADS = 4
GLA_DK = 64
GLA_DV = 128
GLA_GATE_RANK = 16
GLA_GATE_TEMP = 16.0
GLA_CHUNK = 64
GLA_SUB = 8
GLA_STEP = 4
GLA_QK = GLA_HEADS * GLA_DK
GLA_V = GLA_HEADS * GLA_DV

ATT_Q_HEADS = 8
ATT_KV_HEADS = 2
ATT_HD = 64
ATT_WINDOW = 128
ATT_BLOCK = 128
ROPE_BASE = 10000.0
ATT_Q = ATT_Q_HEADS * ATT_HD
ATT_KV = ATT_KV_HEADS * ATT_HD

SSD_HEADS = 16
SSD_HD = 64
SSD_GROUPS = 2
SSD_STATE = 128
SSD_CONV = 5
SSD_CHUNK = 128
SSD_STEP = 2
SSD_INNER = SSD_HEADS * SSD_HD
SSD_BC = SSD_GROUPS * SSD_STATE
SSD_XBC = SSD_INNER + 2 * SSD_BC
SSD_HPG = SSD_HEADS // SSD_GROUPS

N_BRANCH = 3
N_EXPERTS = 16
N_EXPERT_GROUPS = 4
EXPERTS_PER_GROUP = N_EXPERTS // N_EXPERT_GROUPS
EXPERT_FF = 1024

IN_NAMES = ('gla_q', 'gla_k', 'gla_v', 'gla_r', 'gla_gf', 'gla_gb', 'att_q', 'att_k', 'att_v',
            'ssd_z', 'ssd_xbc', 'ssd_dtf', 'ssd_dtb', 'merge')
IN_SPLITS = (GLA_QK, GLA_QK, GLA_V, GLA_V, GLA_GATE_RANK, GLA_GATE_RANK, ATT_Q, ATT_KV, ATT_KV,
             SSD_INNER, SSD_XBC, SSD_HEADS, SSD_HEADS, N_BRANCH * D_MODEL)

P_ORDER = ('merge', 'ssd_xbc', 'gla_v', 'gla_r', 'att_q', 'ssd_z', 'gla_q', 'gla_k', 'att_k', 'att_v',
           'gla_gf', 'gla_gb', 'ssd_dtf', 'ssd_dtb')
LANE = 128
ID_LANES = LANE
SMALL_PAD = 4 * 16
N_P = sum(IN_SPLITS) + SMALL_PAD
H_EXT = D_MODEL + ID_LANES


def _p_offsets():
    width = dict(zip(IN_NAMES, IN_SPLITS))
    off, o = {}, 0
    for name in P_ORDER:
        off[name] = o
        o += width[name]
    return off


P_OFF = _p_offsets()
SMALL_OFF = P_OFF['gla_gf']
SM_GF, SM_GB, SM_DTF, SM_DTB = 0, 16, 32, 48

ROW_TILE = 256
MOE_BLOCK = 256
NEG = -1e30
LOG2E = 1.4426950408889634
DMA_UNROLL = 8
VMEM_LIMIT = 56 * 1024 * 1024

NN = (((1,), (0,)), ((), ()))
NT = (((1,), (1,)), ((), ()))
TN = (((0,), (0,)), ((), ()))


def _dot(a, b, dims=NN):
    return lax.dot_general(a.astype(BF16), b.astype(BF16), dims, preferred_element_type=F32)


def _dot_hi(a, b, dims=NN):
    return lax.dot_general(a.astype(F32), b.astype(F32), dims, precision=lax.Precision.HIGHEST,
                           preferred_element_type=F32)


def _split(x, terms):
    parts, rem = [], x
    for i in range(terms):
        part = rem.astype(BF16)
        parts.append(part)
        if i + 1 < terms:
            rem = rem - part.astype(F32)
    return parts


def _dot_exact_rhs(m01, x, terms, dims=NN):
    m01 = m01.astype(BF16)
    return sum(lax.dot_general(m01, part, dims, preferred_element_type=F32) for part in _split(x, terms))


def _dot_exact_lhs(x, m01, terms, dims=NN):
    m01 = m01.astype(BF16)
    return sum(lax.dot_general(part, m01, dims, preferred_element_type=F32) for part in _split(x, terms))


def _dot3(a, b, dims=NN):
    (ah, al), (bh, bl) = _split(a, 2), _split(b, 2)
    mm = lambda x, y: lax.dot_general(x, y, dims, preferred_element_type=F32)
    return mm(ah, bh) + mm(ah, bl) + mm(al, bh)


def _sigmoid(x):
    return 1.0 / (1.0 + jnp.exp(-x))


def _silu(x):
    return x * _sigmoid(x)


def _softplus(x):
    return jnp.maximum(x, 0.0) + jnp.log(1.0 + jnp.exp(-jnp.abs(x)))


def _iota(shape, dim):
    return lax.broadcasted_iota(jnp.int32, shape, dim)


def _cparams(n_axes):
    return pltpu.CompilerParams(dimension_semantics=("arbitrary",) * n_axes, vmem_limit_bytes=VMEM_LIMIT)


def _largest_tile(n, cap, mult):
    t = (min(cap, n) // mult) * mult
    while n % t:
        t -= mult
    return t


def _mod_kernel(c_ref, w_ref, b_ref, o_ref):
    o_ref[0] = _dot_hi(_silu(c_ref[...]), w_ref[0]) + b_ref[0]


def _mod_vectors(cc, w_mod, b_mod):
    n_l, d, n6 = w_mod.shape
    tn = 1024
    return pl.pallas_call(
        _mod_kernel,
        grid=(n_l, n6 // tn),
        in_specs=[pl.BlockSpec(cc.shape, lambda l, j: (0, 0)),
                  pl.BlockSpec((1, d, tn), lambda l, j: (l, 0, j)),
                  pl.BlockSpec((1, 1, tn), lambda l, j: (l, 0, j))],
        out_specs=pl.BlockSpec((1, cc.shape[0], tn), lambda l, j: (l, 0, j)),
        out_shape=jax.ShapeDtypeStruct((n_l, cc.shape[0], n6), F32),
        compiler_params=_cparams(2),
        name="mod_vectors",
    )(cc, w_mod, b_mod.reshape(n_l, 1, n6))


def _mod_spec(which, nct):
    return pl.BlockSpec((1, 1, D_MODEL), lambda b, j: (2 * b + jnp.where(j >= nct, 1, 0), 0, which))


def _normmod(x, g, shift, scale):
    y = x * lax.rsqrt(jnp.mean(x * x, axis=-1, keepdims=True) + EPS) * g
    return y * (1.0 + scale) + shift


def _normmod_kernel(x_ref, g_ref, sh_ref, sc_ref, o_ref):
    o_ref[...] = _normmod(x_ref[...], g_ref[...], sh_ref[0], sc_ref[0]).astype(o_ref.dtype)


def _norm_modulate(xc, g, mod, bsz, nt_rows, nct):
    tiles = nt_rows // ROW_TILE
    return pl.pallas_call(
        _normmod_kernel,
        grid=(bsz, tiles),
        in_specs=[pl.BlockSpec((ROW_TILE, D_MODEL), lambda b, j: (b * tiles + j, 0)),
                  pl.BlockSpec((1, D_MODEL), lambda b, j: (0, 0)),
                  _mod_spec(0, nct), _mod_spec(1, nct)],
        out_specs=pl.BlockSpec((ROW_TILE, D_MODEL), lambda b, j: (b * tiles + j, 0)),
        out_shape=jax.ShapeDtypeStruct(xc.shape, BF16),
        compiler_params=_cparams(2),
        name="norm_modulate",
    )(xc, g.reshape(1, -1), mod, mod)


def _matmul_kernel(a_ref, w_ref, o_ref):
    o_ref[...] = jnp.dot(a_ref[...], w_ref[...], preferred_element_type=F32).astype(o_ref.dtype)


def _in_proj(h, w):
    m, kdim = h.shape
    n = w.shape[1]
    tm = _largest_tile(m, 2048, 256)
    tn = 1152
    return pl.pallas_call(
        _matmul_kernel,
        grid=(m // tm, n // tn),
        in_specs=[pl.BlockSpec((tm, kdim), lambda i, j: (i, 0)),
                  pl.BlockSpec((kdim, tn), lambda i, j: (0, j))],
        out_specs=pl.BlockSpec((tm, tn), lambda i, j: (i, j)),
        out_shape=jax.ShapeDtypeStruct((m, n), F32),
        compiler_params=_cparams(2),
        name="in_proj",
    )(h, w)


def _chunk_order(j, nc_ctx, nc, rev):
    if not rev:
        return j
    return jnp.where(j < nc_ctx, nc_ctx - 1 - j, nc - 1 - (j - nc_ctx))


def _gla_kernel(q_ref, k_ref, v_ref, sm_ref, w2_ref, b2_ref, *rest, rev, final):
    if final:
        of_ref, r_ref, gn_ref, o_ref, st_ref, b_ref = rest
    else:
        o_ref, st_ref, b_ref = rest
    L, S = GLA_CHUNK, GLA_SUB
    rows = q_ref.shape[0]

    @pl.when(pl.program_id(1) == 0)
    def _():
        st_ref[...] = jnp.zeros_like(st_ref)

    logits = _dot3(sm_ref[...], w2_ref[0]) + b2_ref[0]
    g = (jnp.minimum(logits, 0.0) - jnp.log(1.0 + jnp.exp(-jnp.abs(logits)))) * (LOG2E / GLA_GATE_TEMP)
    rr, cc = _iota((rows, rows), 0), _iota((rows, rows), 1)
    tri = jnp.where((rr // L == cc // L) & ((cc >= rr) if rev else (cc <= rr)), 1.0, 0.0)
    b_ref[...] = _dot_exact_rhs(tri, g, 3)

    head_of_lane = _iota((1, GLA_QK), 1) // GLA_DK
    seg = jnp.where(_iota((GLA_QK, GLA_V), 0) // GLA_DK == _iota((GLA_QK, GLA_V), 1) // GLA_DV, 1.0, 0.0)
    sub_row = _iota((S, 1), 0)
    by_head = lambda x: [jnp.where(head_of_lane == h, x, 0.0) for h in range(GLA_HEADS)]
    n_sub = L // S
    order = list(reversed(range(rows // L))) if rev else list(range(rows // L))
    last = 0 if rev else L - 1

    groups = []
    for i in range(L // (2 * S)):
        r0 = 2 * S * i
        if (not rev) and i > 0:
            groups.append((r0, 2 * S, 0, r0, r0 - 1))
        if rev and r0 + 2 * S < L:
            groups.append((r0, 2 * S, r0 + 2 * S, L - r0 - 2 * S, r0 + 2 * S))
        groups.append((r0, S, r0 + S, S, r0 + S) if rev else (r0 + S, S, r0, S, r0 + S - 1))

    vals = {c: (q_ref[L * c:L * (c + 1)] * (GLA_DK ** -0.5), k_ref[L * c:L * (c + 1)],
                v_ref[L * c:L * (c + 1)], b_ref[L * c:L * (c + 1)]) for c in order}

    scores = {}
    for c in order:
        q, k, v, b = vals[c]
        for gi, (q0, nq, k0, nk, edge) in enumerate(groups):
            ref = b[edge:edge + 1]
            qd = q[q0:q0 + nq] * jnp.exp2(b[q0:q0 + nq] - ref)
            kd = k[k0:k0 + nk] * jnp.exp2(ref - b[k0:k0 + nk])
            scores[c, gi] = _dot(jnp.concatenate(by_head(qd), axis=0), kd, NT)

    ps = []
    for c in order:
        q, k, v, b = vals[c]
        for i in range(n_sub):
            qi, ki, bi = q[S * i:S * (i + 1)], k[S * i:S * (i + 1)], b[S * i:S * (i + 1)]
            for s in range(S):
                valid = (sub_row <= s) if rev else (sub_row >= s)
                ps.append(qi * jnp.exp2(jnp.where(valid, bi - bi[s:s + 1], NEG)) * ki[s:s + 1])
    w_all = _dot(jnp.concatenate(ps, axis=0), seg)

    intra = {}
    for ci, c in enumerate(order):
        q, k, v, b = vals[c]
        pieces = [None] * n_sub

        def add(i, val):
            pieces[i] = val if pieces[i] is None else pieces[i] + val

        for gi, (q0, nq, k0, nk, edge) in enumerate(groups):
            s_g = scores[c, gi]
            val = jnp.concatenate([_dot(s_g[nq * h:nq * (h + 1)], v[k0:k0 + nk, GLA_DV * h:GLA_DV * (h + 1)])
                                   for h in range(GLA_HEADS)], axis=1)
            for j in range(nq // S):
                add(q0 // S + j, val[S * j:S * (j + 1)])
        for i in range(n_sub):
            vi = v[S * i:S * (i + 1)]
            base = (ci * n_sub + i) * S * S
            for s in range(S):
                add(i, w_all[base + S * s:base + S * (s + 1)] * vi[s:s + 1])
        intra[c] = jnp.concatenate(pieces, axis=0)

    st = st_ref[...]
    for c in order:
        q, k, v, b = vals[c]
        b_last = b[last:last + 1]
        o = intra[c] + jnp.concatenate([_dot(qh, st, NT) for qh in by_head(q * jnp.exp2(b))], axis=1)
        kk = k * jnp.exp2(b_last - b)
        new = None
        for h in range(GLA_HEADS):
            u = _dot(v[:, GLA_DV * h:GLA_DV * (h + 1)], kk, TN)
            new = u if new is None else jnp.where(head_of_lane == h, u, new)
        st = jnp.exp2(b_last) * st + new
        if final:
            tot = o + of_ref[L * c:L * (c + 1)]
            outs = []
            for h in range(GLA_HEADS):
                oh = tot[:, GLA_DV * h:GLA_DV * (h + 1)]
                outs.append(oh * lax.rsqrt(jnp.mean(oh * oh, axis=-1, keepdims=True) + EPS) * gn_ref[...])
            o_ref[L * c:L * (c + 1)] = (jnp.concatenate(outs, axis=1) * _silu(r_ref[L * c:L * (c + 1)])).astype(o_ref.dtype)
        else:
            o_ref[L * c:L * (c + 1)] = o
    st_ref[...] = st


def _gla_pass(p, w2, b2, bsz, nt_rows, n_ctx, rev, extra=None):
    rows_per_step = GLA_STEP * GLA_CHUNK
    nb, nb_ctx = nt_rows // rows_per_step, n_ctx // rows_per_step

    def rows(width, col):
        return pl.BlockSpec((rows_per_step, width), lambda b, j: (b * nb + _chunk_order(j, nb_ctx, nb, rev), col))

    d = 1 if rev else 0
    in_specs = [rows(GLA_QK, P_OFF['gla_q'] // GLA_QK), rows(GLA_QK, P_OFF['gla_k'] // GLA_QK),
                rows(GLA_V, P_OFF['gla_v'] // GLA_V), rows(LANE, SMALL_OFF // LANE),
                pl.BlockSpec((1, LANE, GLA_QK), lambda b, j: (d, 0, 0)),
                pl.BlockSpec((1, 1, GLA_QK), lambda b, j: (d, 0, 0))]
    w2_lanes = jnp.zeros((2, LANE, GLA_QK), F32)
    w2_lanes = w2_lanes.at[0, SM_GF:SM_GF + GLA_GATE_RANK].set(w2[0]).at[1, SM_GB:SM_GB + GLA_GATE_RANK].set(w2[1])
    args = [p, p, p, p, w2_lanes, b2.reshape(2, 1, GLA_QK)]
    final = extra is not None
    if final:
        o_first, gn = extra
        in_specs += [rows(GLA_V, 0), rows(GLA_V, P_OFF['gla_r'] // GLA_V),
                     pl.BlockSpec((1, GLA_DV), lambda b, j: (0, 0))]
        args += [o_first, p, gn.reshape(1, GLA_DV)]
    return pl.pallas_call(
        functools.partial(_gla_kernel, rev=rev, final=final),
        grid=(bsz, nb),
        in_specs=in_specs,
        out_specs=rows(GLA_V, 0),
        out_shape=jax.ShapeDtypeStruct((bsz * nt_rows, GLA_V), BF16 if final else F32),
        scratch_shapes=[pltpu.VMEM((GLA_DV, GLA_QK), F32), pltpu.VMEM((rows_per_step, GLA_QK), F32)],
        compiler_params=_cparams(2),
        name="gla_bwd_out" if final else "gla_fwd",
    )(*args)


def _att_kernel(sink_ref, q_ref, kp_ref, kc_ref, kn_ref, vp_ref, vc_ref, vn_ref,
                cp_ref, sp_ref, cc_ref, sc_ref, cn_ref, sn_ref, kx_ref, vx_ref, qg_ref, kg_ref,
                o_ref, *, nb_ctx, nb_lat, n_ctx):
    j = pl.program_id(1)
    blk = ATT_BLOCK
    lane = _iota((1, LANE), 1)
    half_mean = jnp.where(_iota((LANE, LANE), 0) // ATT_HD == _iota((LANE, LANE), 1) // ATT_HD,
                          1.0 / ATT_HD, 0.0)

    def norm(x, g_ref):
        return x * lax.rsqrt(_dot(x * x, half_mean) + EPS) * g_ref[...]

    def rope(x, c_ref, s_ref):
        swapped = jnp.where(lane % 32 < 16, pltpu.roll(x, LANE - 16, 1), pltpu.roll(x, 16, 1))
        return x * c_ref[...] + swapped * s_ref[...]

    def attend(qpairs, keys, vals, bias):
        keys_sw = pltpu.roll(keys, ATT_HD, 1).astype(BF16)
        vals_sw = pltpu.roll(vals, ATT_HD, 1).astype(BF16)
        keys, vals = keys.astype(BF16), vals.astype(BF16)
        rep = ATT_Q_HEADS // ATT_KV_HEADS
        heads = range(ATT_Q_HEADS)
        straight = [(h % 2) == (h // rep) for h in heads]
        scores = []
        for h in heads:
            qm = jnp.where((lane >= ATT_HD) if h % 2 else (lane < ATT_HD), qpairs[h // 2], 0.0)
            s = _dot(qm, keys if straight[h] else keys_sw, NT)
            if bias is not None:
                for c0, add in bias:
                    parts = [s[:, :c0], s[:, c0:c0 + blk] + add, s[:, c0 + blk:]]
                    s = jnp.concatenate([part for part in parts if part.shape[1]], axis=1)
            scores.append(s)
        probs, dens = [], []
        for h in heads:
            sk = sink_ref[0, h] * LOG2E
            m = jnp.maximum(jnp.max(scores[h], axis=-1, keepdims=True), sk)
            pr = jnp.exp2(scores[h] - m)
            probs.append(pr)
            dens.append(jnp.sum(pr, axis=-1, keepdims=True) + jnp.exp2(sk - m))
        outs = [_dot(probs[h], vals if straight[h] else vals_sw) / dens[h] for h in heads]
        o_ref[...] = jnp.concatenate([jnp.where(lane < ATT_HD, outs[2 * pi], outs[2 * pi + 1])
                                      for pi in range(ATT_Q_HEADS // 2)], axis=1).astype(o_ref.dtype)

    kx = norm(kx_ref[...], kg_ref)
    vx = vx_ref[...]
    q_scale = ATT_HD ** -0.5 * LOG2E

    @pl.when(j < nb_ctx)
    def _():
        qpairs = [norm(q_ref[:, LANE * pi:LANE * (pi + 1)], qg_ref) * q_scale
                  for pi in range(ATT_Q_HEADS // 2)]
        attend(qpairs, kx, vx, None)

    @pl.when(j >= nb_ctx)
    def _():
        li = j - nb_ctx
        qpairs = [rope(norm(q_ref[:, LANE * pi:LANE * (pi + 1)], qg_ref), cc_ref, sc_ref) * q_scale
                  for pi in range(ATT_Q_HEADS // 2)]
        keys = jnp.concatenate([kx, rope(norm(kp_ref[...], kg_ref), cp_ref, sp_ref),
                                rope(norm(kc_ref[...], kg_ref), cc_ref, sc_ref),
                                rope(norm(kn_ref[...], kg_ref), cn_ref, sn_ref)], axis=0)
        vals = jnp.concatenate([vx, vp_ref[...], vc_ref[...], vn_ref[...]], axis=0)
        a, jj = _iota((blk, blk), 0), _iota((blk, blk), 1)
        prev_ok = (jj >= a) & (li > 0)
        next_ok = (jj <= a) & (li < nb_lat - 1)
        bias = [(n_ctx, jnp.where(prev_ok, 0.0, NEG)), (n_ctx + 2 * blk, jnp.where(next_ok, 0.0, NEG))]
        attend(qpairs, keys, vals, bias)


def _rope_tables(n):
    rows = n // GRID_W
    row = jnp.repeat(jnp.arange(rows), GRID_W).astype(F32)
    col = jnp.tile(jnp.arange(GRID_W), rows).astype(F32)
    axis_dim = ATT_HD // 2
    inv_freq = ROPE_BASE ** (-jnp.arange(0, axis_dim, 2, dtype=F32) / axis_dim)
    ang_r, ang_c = row[:, None] * inv_freq, col[:, None] * inv_freq
    cos = jnp.concatenate([jnp.cos(ang_r), jnp.cos(ang_r), jnp.cos(ang_c), jnp.cos(ang_c)], axis=1)
    sin = jnp.concatenate([-jnp.sin(ang_r), jnp.sin(ang_r), -jnp.sin(ang_c), jnp.sin(ang_c)], axis=1)
    return jnp.tile(cos, (1, LANE // ATT_HD)), jnp.tile(sin, (1, LANE // ATT_HD))


def _attention(p, cos, sin, sink, qg, kg, bsz, nt_rows, n_ctx):
    blk = ATT_BLOCK
    nb, nb_ctx = nt_rows // blk, n_ctx // blk
    nb_lat = nb - nb_ctx
    k_col, v_col = P_OFF['att_k'] // LANE, P_OFF['att_v'] // LANE

    def lat(j, d):
        return jnp.clip(j - nb_ctx + d, 0, nb_lat - 1)

    def kv(col, d):
        return pl.BlockSpec((blk, LANE), lambda b, j: (b * nb + nb_ctx + lat(j, d), col))

    def tab(d):
        return pl.BlockSpec((blk, LANE), lambda b, j: (lat(j, d), 0))

    ctx_rows = nt_rows // n_ctx
    in_specs = [pl.BlockSpec(memory_space=pltpu.SMEM),
                pl.BlockSpec((blk, ATT_Q), lambda b, j: (b * nb + j, P_OFF['att_q'] // ATT_Q)),
                kv(k_col, -1), kv(k_col, 0), kv(k_col, 1), kv(v_col, -1), kv(v_col, 0), kv(v_col, 1),
                tab(-1), tab(-1), tab(0), tab(0), tab(1), tab(1),
                pl.BlockSpec((n_ctx, LANE), lambda b, j: (b * ctx_rows, k_col)),
                pl.BlockSpec((n_ctx, LANE), lambda b, j: (b * ctx_rows, v_col)),
                pl.BlockSpec((1, LANE), lambda b, j: (0, 0)),
                pl.BlockSpec((1, LANE), lambda b, j: (0, 0))]
    tile2 = lambda g: jnp.tile(g.reshape(1, ATT_HD), (1, LANE // ATT_HD))
    return pl.pallas_call(
        functools.partial(_att_kernel, nb_ctx=nb_ctx, nb_lat=nb_lat, n_ctx=n_ctx),
        grid=(bsz, nb),
        in_specs=in_specs,
        out_specs=pl.BlockSpec((blk, ATT_Q), lambda b, j: (b * nb + j, 0)),
        out_shape=jax.ShapeDtypeStruct((bsz * nt_rows, ATT_Q), BF16),
        compiler_params=_cparams(2),
        name="attention",
    )(sink.reshape(1, ATT_Q_HEADS), p, p, p, p, p, p, p, cos, sin, cos, sin, cos, sin, p, p, tile2(qg), tile2(kg))


def _conv_kernel(xp_ref, xc_ref, xn_ref, w_ref, b_ref, o_ref, ext_ref, *, nct, nt):
    j = pl.program_id(1)
    rows = xc_ref.shape[0]
    no_prev = (j == 0) | (j == nct)
    no_next = (j == nct - 1) | (j == nt - 1)
    ext_ref[0:8] = jnp.where(no_prev, 0.0, xp_ref[...])
    ext_ref[8:8 + rows] = xc_ref[...]
    ext_ref[8 + rows:16 + rows] = jnp.where(no_next, 0.0, xn_ref[...])
    pad = SSD_CONV // 2
    ext = ext_ref[...]
    acc = b_ref[...] + w_ref[pad:pad + 1] * ext[8:8 + rows]
    for t in range(SSD_CONV):
        if t != pad:
            acc = acc + w_ref[t:t + 1] * pltpu.roll(ext, (pad - t) % ext.shape[0], 0)[8:8 + rows]
    o_ref[...] = _silu(acc)


def _ssd_conv(p, w, bias, bsz, nt_rows, n_ctx):
    tiles, nct = nt_rows // ROW_TILE, n_ctx // ROW_TILE
    ct = SSD_XBC
    c0 = P_OFF['ssd_xbc'] // ct
    r8 = ROW_TILE // 8
    last8 = bsz * nt_rows // 8 - 1
    return pl.pallas_call(
        functools.partial(_conv_kernel, nct=nct, nt=tiles),
        grid=(bsz, tiles, SSD_XBC // ct),
        in_specs=[pl.BlockSpec((8, ct), lambda b, j, c: (jnp.maximum((b * tiles + j) * r8 - 1, 0), c0 + c)),
                  pl.BlockSpec((ROW_TILE, ct), lambda b, j, c: (b * tiles + j, c0 + c)),
                  pl.BlockSpec((8, ct), lambda b, j, c: (jnp.minimum((b * tiles + j + 1) * r8, last8), c0 + c)),
                  pl.BlockSpec((SSD_CONV, ct), lambda b, j, c: (0, c)),
                  pl.BlockSpec((1, ct), lambda b, j, c: (0, c))],
        out_specs=pl.BlockSpec((ROW_TILE, ct), lambda b, j, c: (b * tiles + j, c)),
        out_shape=jax.ShapeDtypeStruct((bsz * nt_rows, SSD_XBC), F32),
        scratch_shapes=[pltpu.VMEM((ROW_TILE + 16, ct), F32)],
        compiler_params=_cparams(3),
        name="ssd_conv",
    )(p, p, p, w, bias.reshape(1, -1))


def _ssd_kernel(x_ref, bc_ref, sm_ref, dtb_ref, dtbc_ref, al_ref, alc_ref, *rest, rev, final):
    if final:
        yf_ref, z_ref, d_ref, gn_ref, o_ref, st_ref = rest
    else:
        o_ref, st_ref = rest
    L, P, N = SSD_CHUNK, SSD_HD, SSD_STATE
    gw = SSD_HPG * P
    rows = x_ref.shape[0]

    @pl.when(pl.program_id(1) == 0)
    def _():
        st_ref[...] = jnp.zeros_like(st_ref)

    off = SM_DTB if rev else SM_DTF
    tt, ss = _iota((L, L), 0), _iota((L, L), 1)
    causal = (ss >= tt) if rev else (ss <= tt)
    tri = jnp.where(causal, 1.0, 0.0)
    last = 0 if rev else L - 1
    expand = jnp.where(_iota((LANE, SSD_INNER), 0) - off == _iota((LANE, SSD_INNER), 1) // P, 1.0, 0.0)
    lane = _iota((1, LANE), 1)

    def chunk(c0, st):
        xs = x_ref[c0:c0 + L]
        bm, cm = bc_ref[c0:c0 + L, 0:SSD_BC], bc_ref[c0:c0 + L, SSD_BC:2 * SSD_BC]
        sm = sm_ref[c0:c0 + L]
        dt = _softplus(sm + dtb_ref[0])
        a = -dt * jnp.exp(al_ref[0])
        dt_t = _softplus(sm.T + dtbc_ref[0])
        a_t = -dt_t * jnp.exp(alc_ref[0])
        cum = _dot_exact_rhs(tri, a, 2)
        cum_t = _dot_exact_lhs(a_t, tri, 2, NT)
        cum_last = cum[last:last + 1]
        ecum = _dot(jnp.exp(cum), expand)
        wgt = _dot(jnp.exp(cum_last - cum) * dt, expand)

        ys = []
        for g in range(SSD_GROUPS):
            cg, bg = cm[:, N * g:N * (g + 1)], bm[:, N * g:N * (g + 1)]
            cb = _dot(cg, bg, NT)
            for jp in range(SSD_HPG // 2):
                h0 = SSD_HPG * g + 2 * jp
                ms = []
                for h in (off + h0, off + h0 + 1):
                    dec = jnp.exp(jnp.where(causal, cum[:, h:h + 1] - cum_t[h:h + 1, :], NEG))
                    ms.append(cb * dec * dt_t[h:h + 1, :])
                xp = xs[:, P * h0:P * h0 + LANE]
                rhs = jnp.concatenate([jnp.where(lane < P, xp, 0.0), jnp.where(lane >= P, xp, 0.0)], axis=0)
                ys.append(_dot(jnp.concatenate(ms, axis=1), rhs))
        y = jnp.concatenate(ys, axis=1)
        y = y + jnp.concatenate([_dot(cm[:, N * g:N * (g + 1)], st[:, gw * g:gw * (g + 1)])
                                 for g in range(SSD_GROUPS)], axis=1) * ecum
        xw = xs * wgt
        new = jnp.concatenate([_dot(bm[:, N * g:N * (g + 1)], xw[:, gw * g:gw * (g + 1)], TN)
                               for g in range(SSD_GROUPS)], axis=1)
        return y, xs, ecum[last:last + 1] * st + new

    st = st_ref[...]
    for c in (reversed(range(rows // L)) if rev else range(rows // L)):
        c0 = L * c
        y, xs, st = chunk(c0, st)
        if final:
            tot = (y + yf_ref[c0:c0 + L] + d_ref[...] * xs) * _silu(z_ref[c0:c0 + L])
            outs = []
            for g in range(SSD_GROUPS):
                tg = tot[:, gw * g:gw * (g + 1)]
                outs.append(tg * lax.rsqrt(jnp.mean(tg * tg, axis=-1, keepdims=True) + EPS))
            o_ref[c0:c0 + L] = (jnp.concatenate(outs, axis=1) * gn_ref[...]).astype(o_ref.dtype)
        else:
            o_ref[c0:c0 + L] = y
    st_ref[...] = st


def _ssd_pass(p, xbc, dt_bias, a_log, bsz, nt_rows, n_ctx, rev, extra=None):
    L = SSD_STEP * SSD_CHUNK
    nc, nc_ctx = nt_rows // L, n_ctx // L

    def rows(width, col):
        return pl.BlockSpec((L, width), lambda b, j: (b * nc + _chunk_order(j, nc_ctx, nc, rev), col))

    d = 1 if rev else 0
    vec = pl.BlockSpec((1, 1, LANE), lambda b, j: (d, 0, 0))
    colv = pl.BlockSpec((1, LANE, 1), lambda b, j: (d, 0, 0))
    in_specs = [rows(SSD_INNER, 0), rows(2 * SSD_BC, SSD_INNER // (2 * SSD_BC)), rows(LANE, SMALL_OFF // LANE),
                vec, colv, vec, colv]

    def lanes(v):
        out = jnp.zeros((2, LANE), F32)
        return out.at[0, SM_DTF:SM_DTF + SSD_HEADS].set(v[0]).at[1, SM_DTB:SM_DTB + SSD_HEADS].set(v[1])

    dtb, alog = lanes(dt_bias), lanes(a_log)
    args = [xbc, xbc, p, dtb.reshape(2, 1, LANE), dtb.reshape(2, LANE, 1),
            alog.reshape(2, 1, LANE), alog.reshape(2, LANE, 1)]
    final = extra is not None
    if final:
        y_first, d_skip, gn = extra
        full = pl.BlockSpec((1, SSD_INNER), lambda b, j: (0, 0))
        in_specs += [rows(SSD_INNER, 0), rows(SSD_INNER, P_OFF['ssd_z'] // SSD_INNER), full, full]
        args += [y_first, p, jnp.repeat(d_skip, SSD_HD).reshape(1, SSD_INNER), gn.reshape(1, SSD_INNER)]
    return pl.pallas_call(
        functools.partial(_ssd_kernel, rev=rev, final=final),
        grid=(bsz, nc),
        in_specs=in_specs,
        out_specs=rows(SSD_INNER, 0),
        out_shape=jax.ShapeDtypeStruct((bsz * nt_rows, SSD_INNER), BF16 if final else F32),
        scratch_shapes=[pltpu.VMEM((SSD_STATE, SSD_INNER), F32)],
        compiler_params=_cparams(2),
        name="ssd_bwd_out" if final else "ssd_fwd",
    )(*args)


def _route(h2, rw_ref, rb_ref, run_ref):
    rows = h2.shape[0]
    scores = _sigmoid(_dot3(h2, rw_ref[...]).T[0:N_EXPERTS])
    biased = scores + rb_ref[...]
    row = lambda a, e: a[e:e + 1, :]
    epg = EXPERTS_PER_GROUP
    best = gidx = None
    for g in range(N_EXPERT_GROUPS):
        vals = [row(biased, epg * g + i) for i in range(epg)]
        top2 = None
        for i in range(epg):
            for i2 in range(i + 1, epg):
                pair = vals[i] + vals[i2]
                top2 = pair if top2 is None else jnp.maximum(top2, pair)
        if g == 0:
            best, gidx = top2, jnp.zeros(top2.shape, jnp.int32)
        else:
            take = top2 > best
            best, gidx = jnp.where(take, top2, best), jnp.where(take, g, gidx)

    def pick(a, i):
        out = row(a, i)
        for g in range(1, N_EXPERT_GROUPS):
            out = jnp.where(gidx == g, row(a, epg * g + i), out)
        return out

    cand = [pick(biased, i) for i in range(epg)]
    raw = [pick(scores, i) for i in range(epg)]

    def argmax_first(vals, exclude):
        bv = bi = bs = None
        for i in range(epg):
            v = vals[i] if exclude is None else jnp.where(exclude == i, -jnp.inf, vals[i])
            if bv is None:
                bv, bi, bs = v, jnp.zeros(v.shape, jnp.int32), raw[0]
            else:
                take = v > bv
                bv, bi, bs = jnp.where(take, v, bv), jnp.where(take, i, bi), jnp.where(take, raw[i], bs)
        return bi, bs

    i0, s0 = argmax_first(cand, None)
    i1, s1 = argmax_first(cand, i0)
    e0, e1 = gidx * epg + i0, gidx * epg + i1
    tot = s0 + s1

    eid = _iota((N_EXPERTS, rows), 0)
    oh0 = jnp.where(eid == e0, 1.0, 0.0)
    oh1 = jnp.where(eid == e1, 1.0, 0.0)
    cnt = oh0 + oh1
    before = jnp.where(_iota((rows, rows), 0) < _iota((rows, rows), 1), 1.0, 0.0)
    prefix = _dot(cnt, before) + run_ref[:, 0:1]
    r0 = jnp.sum(oh0 * prefix, axis=0, keepdims=True).astype(jnp.int32)
    r1 = jnp.sum(oh1 * prefix, axis=0, keepdims=True).astype(jnp.int32)
    run_ref[...] = run_ref[...] + jnp.sum(cnt, axis=1, keepdims=True)
    return (e0, e1), (s0 / tot, s1 / tot), (r0, r1)


def _merge_kernel(yg_ref, ya_ref, ys_ref, mg_ref, x_ref, gb_ref, g1_ref, sh2_ref, sc2_ref, n2_ref,
                  wa_ref, wb_ref, wc_ref, wo_ref, rw_ref, rb_ref,
                  xo_ref, hx_ref, e_ref, g_ref, r_ref, c_ref, run_ref):
    d = D_MODEL
    rows = x_ref.shape[0]
    tile = pl.program_id(0) * pl.num_programs(1) + pl.program_id(1)

    @pl.when(tile == 0)
    def _():
        run_ref[...] = jnp.zeros_like(run_ref)

    gates = _sigmoid(mg_ref[...] + gb_ref[...])
    merged = (gates[:, 0:d] * _dot(yg_ref[...], wa_ref[...])
              + gates[:, d:2 * d] * _dot(ya_ref[...], wb_ref[...])
              + gates[:, 2 * d:3 * d] * _dot(ys_ref[...], wc_ref[...]))
    xn = x_ref[...] + g1_ref[0] * _dot(merged, wo_ref[...])
    xo_ref[...] = xn
    h2 = _normmod(xn, n2_ref[...], sh2_ref[0], sc2_ref[0])
    (e0, e1), (g0, g1), (r0, r1) = _route(h2, rw_ref, rb_ref, run_ref)
    e_ref[...] = jnp.concatenate([e0, e1], axis=0)
    g_ref[...] = jnp.concatenate([g0, g1], axis=0)
    r_ref[...] = jnp.concatenate([r0, r1], axis=0)
    c_ref[...] = run_ref[...].astype(jnp.int32)
    token1 = (tile * rows + _iota((rows, 1), 0) + 1).astype(F32)
    first_expert = jnp.broadcast_to(e0.astype(F32), (ID_LANES, rows)).T
    hx_ref[:, 0:d] = h2
    hx_ref[:, d:d + ID_LANES] = jnp.where(_iota((1, ID_LANES), 1) == 0, token1, first_expert)


def _merge(y_gla, y_att, y_ssd, p, xc, gate_b, mod, norm2_g, wa, wb, wc, wo, router_w, router_b, bsz, nt_rows, nct):
    tiles = nt_rows // ROW_TILE
    d = D_MODEL
    t = xc.shape[0]

    def rows(width, col=0):
        return pl.BlockSpec((ROW_TILE, width), lambda b, j: (b * tiles + j, col))

    def whole(a):
        return pl.BlockSpec(a.shape, lambda b, j: (0,) * a.ndim)

    gb = gate_b.reshape(1, N_BRANCH * d)
    n2 = norm2_g.reshape(1, d)
    rw = jnp.pad(router_w, ((0, 0), (0, LANE - N_EXPERTS)))
    rb = router_b.reshape(N_EXPERTS, 1)
    lanes = pl.BlockSpec((2, ROW_TILE), lambda b, j: (0, b * tiles + j))
    return pl.pallas_call(
        _merge_kernel,
        grid=(bsz, tiles),
        in_specs=[rows(GLA_V), rows(ATT_Q), rows(SSD_INNER), rows(N_BRANCH * d, P_OFF['merge'] // (N_BRANCH * d)),
                  rows(d), whole(gb), _mod_spec(2, nct), _mod_spec(3, nct), _mod_spec(4, nct), whole(n2),
                  whole(wa), whole(wb), whole(wc), whole(wo), whole(rw), whole(rb)],
        out_specs=[rows(d), rows(H_EXT), lanes, lanes, lanes, pl.BlockSpec((N_EXPERTS, LANE), lambda b, j: (0, 0))],
        out_shape=[jax.ShapeDtypeStruct((t, d), F32), jax.ShapeDtypeStruct((t, H_EXT), F32),
                   jax.ShapeDtypeStruct((2, t), jnp.int32), jax.ShapeDtypeStruct((2, t), F32),
                   jax.ShapeDtypeStruct((2, t), jnp.int32), jax.ShapeDtypeStruct((N_EXPERTS, LANE), jnp.int32)],
        scratch_shapes=[pltpu.VMEM((N_EXPERTS, LANE), F32)],
        compiler_params=_cparams(2),
        name="merge",
    )(y_gla, y_att, y_ssd, p, xc, gb, mod, mod, mod, n2, wa, wb, wc, wo, rw, rb)


def _dispatch_kernel(pe_ref, dest_ref, h_ref, buf_ref, zero_ref, sem, *, n_slots):
    rows = h_ref.shape[0]

    @pl.when(pl.program_id(0) == 0)
    def _():
        zero_ref[...] = jnp.zeros_like(zero_ref)

        def zero_block(first):
            def one(r, carry):
                pltpu.make_async_copy(zero_ref.at[pl.ds(0, 1)], buf_ref.at[pl.ds(first + r, 1)], sem).start()
                return carry
            lax.fori_loop(0, rows, one, 0)
            pltpu.make_async_copy(h_ref, buf_ref.at[pl.ds(0, rows)], sem).wait()

        def per_expert(e, carry):
            zero_block(jnp.maximum(pe_ref[e] - rows, 0))
            zero_block(n_slots - (e + 1) * rows)
            return carry

        lax.fori_loop(0, N_EXPERTS, per_expert, 0)

    def row_copy(r, k):
        return pltpu.make_async_copy(h_ref.at[pl.ds(r, 1)], buf_ref.at[pl.ds(dest_ref[k, r], 1)], sem)

    def start(r, carry):
        row_copy(r, 0).start()
        row_copy(r, 1).start()
        return carry

    lax.fori_loop(0, rows, start, 0, unroll=DMA_UNROLL)
    for _ in range(2):
        pltpu.make_async_copy(h_ref, buf_ref.at[pl.ds(0, rows)], sem).wait()


def _dispatch(hx, dest, pad_end, n_slots):
    t, w = hx.shape
    assert ROW_TILE == MOE_BLOCK
    grid_spec = pltpu.PrefetchScalarGridSpec(
        num_scalar_prefetch=1,
        grid=(t // ROW_TILE,),
        in_specs=[pl.BlockSpec((2, ROW_TILE), lambda i, pe: (0, i), memory_space=pltpu.SMEM),
                  pl.BlockSpec((ROW_TILE, w), lambda i, pe: (i, 0))],
        out_specs=pl.BlockSpec(memory_space=pl.ANY),
        scratch_shapes=[pltpu.VMEM((8, w), F32), pltpu.SemaphoreType.DMA(())],
    )
    return pl.pallas_call(
        functools.partial(_dispatch_kernel, n_slots=n_slots),
        grid_spec=grid_spec,
        out_shape=jax.ShapeDtypeStruct((n_slots, w), F32),
        compiler_params=_cparams(1),
        name="moe_dispatch",
    )(pad_end.astype(jnp.int32), dest, hx)


def _expert_kernel(be_ref, real_ref, x_ref, w1_ref, w3_ref, w2_ref, o2_ref, obuf, tvec, tsm, sem_t, sem_o, *,
                   n_rows2, n_slots):
    i = pl.program_id(0)
    rows = x_ref.shape[0]
    cur = i % 2
    prv = 1 - cur
    d = D_MODEL

    def send(slot, r):
        return pltpu.make_async_copy(obuf.at[slot, pl.ds(r, 1)], o2_ref.at[pl.ds(tsm[slot, 0, r], 1)], sem_o.at[slot])

    def send_row(slot, r, carry):
        send(slot, r).start()
        return carry

    def drain(slot):
        pltpu.make_async_copy(obuf.at[slot], o2_ref.at[pl.ds(0, rows)], sem_o.at[slot]).wait()

    @pl.when(i == 0)
    def _():
        obuf[...] = jnp.zeros_like(obuf)

        def spare(r, carry):
            tsm[0, 0, r] = n_slots + r
            tsm[1, 0, r] = n_slots + rows + r
            return carry

        lax.fori_loop(0, rows, spare, 0)
        lax.fori_loop(0, rows, functools.partial(send_row, 0), 0)

    drain(cur)

    origin = x_ref[:, d:d + ID_LANES].T
    token1 = origin[0:1].astype(jnp.int32)
    plane = jnp.where(origin[1:2].astype(jnp.int32) == be_ref[i], 0, n_rows2 // 2)
    empty_row = n_rows2 + (i * rows - real_ref[i]) + _iota((1, rows), 1)
    tvec[...] = jnp.broadcast_to(jnp.where(token1 > 0, plane + token1 - 1, empty_row), tvec.shape)
    to_smem = pltpu.make_async_copy(tvec, tsm.at[cur], sem_t)
    to_smem.start()

    third = rows // 3
    for r in range(third):
        send(prv, r).start()
    x = x_ref[:, 0:d].astype(BF16)
    obuf[cur] = _dot(x, w1_ref[0])
    for r in range(third, 2 * third):
        send(prv, r).start()
    obuf[cur] = _silu(obuf[cur]) * _dot(x, w3_ref[0])
    for r in range(2 * third, rows):
        send(prv, r).start()
    obuf[cur] = _dot(obuf[cur], w2_ref[0])
    to_smem.wait()

    @pl.when(i == pl.num_programs(0) - 1)
    def _():
        lax.fori_loop(0, rows, functools.partial(send_row, cur), 0)
        drain(prv)
        drain(cur)


def _experts(buf, block_e, block_real, w1, w3, w2, n_rows2):
    n_slots, w = buf.shape
    d, ff = w1.shape[1], w1.shape[2]
    grid_spec = pltpu.PrefetchScalarGridSpec(
        num_scalar_prefetch=2,
        grid=(n_slots // MOE_BLOCK,),
        in_specs=[pl.BlockSpec((MOE_BLOCK, w), lambda i, be, br: (i, 0)),
                  pl.BlockSpec((1, d, ff), lambda i, be, br: (be[i], 0, 0)),
                  pl.BlockSpec((1, d, ff), lambda i, be, br: (be[i], 0, 0)),
                  pl.BlockSpec((1, ff, d), lambda i, be, br: (be[i], 0, 0))],
        out_specs=pl.BlockSpec(memory_space=pl.ANY),
        scratch_shapes=[pltpu.VMEM((2, MOE_BLOCK, d), F32), pltpu.VMEM((8, MOE_BLOCK), jnp.int32),
                        pltpu.SMEM((2, 8, MOE_BLOCK), jnp.int32), pltpu.SemaphoreType.DMA(()),
                        pltpu.SemaphoreType.DMA((2,))],
    )
    return pl.pallas_call(
        functools.partial(_expert_kernel, n_rows2=n_rows2, n_slots=n_slots),
        grid_spec=grid_spec,
        out_shape=jax.ShapeDtypeStruct((n_slots + 2 * MOE_BLOCK, d), F32),
        compiler_params=_cparams(1),
        name="moe_experts",
    )(block_e, block_real, buf, w1, w3, w2)


def _combine_kernel(oa_ref, ob_ref, g_ref, x_ref, g2_ref, *rest, with_next):
    y = g_ref[:, 0:1] * oa_ref[...] + g_ref[:, 1:2] * ob_ref[...]
    xo = x_ref[...] + g2_ref[0] * y
    if with_next:
        n1_ref, sh_ref, sc_ref, xo_ref, h_ref = rest
        h_ref[...] = _normmod(xo, n1_ref[...], sh_ref[0], sc_ref[0]).astype(h_ref.dtype)
    else:
        xo_ref, = rest
    xo_ref[...] = xo


def _combine(o2, gates_col, x_new, mod, bsz, nt_rows, nct, latent_only, nxt=None):
    tiles = nt_rows // ROW_TILE
    d = D_MODEL
    skip = nct if latent_only else 0
    out_tiles = tiles - skip
    src = lambda b, j: b * tiles + skip + j
    kind = lambda j: jnp.where(skip + j >= nct, 1, 0)
    second = bsz * tiles
    in_specs = [pl.BlockSpec((ROW_TILE, d), lambda b, j: (src(b, j), 0)),
                pl.BlockSpec((ROW_TILE, d), lambda b, j: (second + src(b, j), 0)),
                pl.BlockSpec((ROW_TILE, 2), lambda b, j: (src(b, j), 0)),
                pl.BlockSpec((ROW_TILE, d), lambda b, j: (src(b, j), 0)),
                pl.BlockSpec((1, 1, d), lambda b, j: (2 * b + kind(j), 0, 5))]
    args = [o2, o2, gates_col, x_new, mod]
    out_rows = pl.BlockSpec((ROW_TILE, d), lambda b, j: (b * out_tiles + j, 0))
    out_specs, out_shape = [out_rows], [jax.ShapeDtypeStruct((bsz * out_tiles * ROW_TILE, d), F32)]
    if nxt is not None:
        norm_g, mod_next = nxt
        in_specs += [pl.BlockSpec((1, d), lambda b, j: (0, 0)),
                     pl.BlockSpec((1, 1, d), lambda b, j: (2 * b + kind(j), 0, 0)),
                     pl.BlockSpec((1, 1, d), lambda b, j: (2 * b + kind(j), 0, 1))]
        args += [norm_g.reshape(1, d), mod_next, mod_next]
        out_specs.append(out_rows)
        out_shape.append(jax.ShapeDtypeStruct((bsz * out_tiles * ROW_TILE, d), BF16))
    return pl.pallas_call(
        functools.partial(_combine_kernel, with_next=nxt is not None),
        grid=(bsz, out_tiles),
        in_specs=in_specs,
        out_specs=out_specs,
        out_shape=out_shape,
        compiler_params=_cparams(2),
        name="moe_combine",
    )(*args)


def _moe(hx, routing, x_new, mod, w1, w3, w2, bsz, nt_rows, nct, latent_only, nxt):
    t = hx.shape[0]
    e_idx, gates, rank, counts = routing
    counts = counts[:, 0]
    padded = (counts + MOE_BLOCK - 1) // MOE_BLOCK * MOE_BLOCK
    pad_end = jnp.cumsum(padded)
    pad_start = pad_end - padded
    first = jnp.sum(jnp.where(e_idx[None] == jnp.arange(N_EXPERTS)[:, None, None], pad_start[:, None, None], 0),
                    axis=0)
    dest = first + rank
    n_blocks = -(-2 * t // MOE_BLOCK) + N_EXPERTS
    block_e = jnp.minimum(jnp.searchsorted(pad_end, jnp.arange(n_blocks, dtype=jnp.int32) * MOE_BLOCK,
                                           side='right'), N_EXPERTS - 1).astype(jnp.int32)
    block_real = jnp.sum(jnp.where(jnp.arange(N_EXPERTS)[None] <= block_e[:, None], counts[None], 0), axis=1)
    assert (2 * t) % MOE_BLOCK == 0
    buf = _dispatch(hx, dest, pad_end, n_blocks * MOE_BLOCK)
    o2 = _experts(buf, block_e, block_real.astype(jnp.int32), w1, w3, w2, 2 * t)
    return _combine(o2, gates.T, x_new, mod, bsz, nt_rows, nct, latent_only, nxt)


def _permute_w_in(w_in):
    offsets = np.concatenate([[0], np.cumsum(IN_SPLITS)])
    start = dict(zip(IN_NAMES, offsets[:-1]))
    width = dict(zip(IN_NAMES, IN_SPLITS))
    cols = [w_in[:, start[n]:start[n] + width[n]] for n in P_ORDER]
    cols.append(jnp.zeros((w_in.shape[0], SMALL_PAD), w_in.dtype))
    return jnp.concatenate(cols, axis=1).astype(BF16)


def _layer(xc, h, mod, lp, router_w, router_b, rope, bsz, nt_rows, n_ctx, nxt):
    nct = n_ctx // ROW_TILE
    if h is None:
        h = _norm_modulate(xc, lp['norm1_g'], mod, bsz, nt_rows, nct)
    p = _in_proj(h, _permute_w_in(lp['w_in']))

    o_f = _gla_pass(p, lp['gla_w2'], lp['gla_b2'], bsz, nt_rows, n_ctx, rev=False)
    y_gla = _gla_pass(p, lp['gla_w2'], lp['gla_b2'], bsz, nt_rows, n_ctx, rev=True,
                      extra=(o_f, lp['gla_norm_g']))
    y_att = _attention(p, rope[0], rope[1], lp['att_sink'], lp['q_norm_g'], lp['k_norm_g'], bsz, nt_rows, n_ctx)
    xbc = _ssd_conv(p, lp['ssd_conv_w'], lp['ssd_conv_b'], bsz, nt_rows, n_ctx)
    y_f = _ssd_pass(p, xbc, lp['ssd_dt_bias'], lp['ssd_a_log'], bsz, nt_rows, n_ctx, rev=False)
    y_ssd = _ssd_pass(p, xbc, lp['ssd_dt_bias'], lp['ssd_a_log'], bsz, nt_rows, n_ctx, rev=True,
                      extra=(y_f, lp['ssd_d'], lp['ssd_norm_g']))

    x_new, hx, *routing = _merge(y_gla, y_att, y_ssd, p, xc, lp['gate_b'], mod, lp['norm2_g'],
                                 lp['w_br_a'].astype(BF16), lp['w_br_b'].astype(BF16), lp['w_br_c'].astype(BF16),
                                 lp['w_out'].astype(BF16), router_w, router_b, bsz, nt_rows, nct)
    return _moe(hx, routing, x_new, mod, lp['w1'].astype(BF16), lp['w3'].astype(BF16), lp['w2'].astype(BF16),
                bsz, nt_rows, nct, latent_only=nxt is None, nxt=nxt)


def kernel(x, c, ctx, c_ctx, w_mod, b_mod, norm1_g, w_in, gla_w2, gla_b2, gla_norm_g, q_norm_g, k_norm_g, att_sink, ssd_conv_w, ssd_conv_b, ssd_dt_bias, ssd_a_log, ssd_d, ssd_norm_g, gate_b, w_br_a, w_br_b, w_br_c, w_out, norm2_g, router_w, router_b, w1, w3, w2):
    bsz, seq, d = x.shape
    n_ctx = ctx.shape[1]
    nt_rows = n_ctx + seq
    assert d == D_MODEL and n_ctx % ROW_TILE == 0 and seq % ROW_TILE == 0 and nt_rows % n_ctx == 0
    assert seq % GRID_W == 0 and bsz < 16

    cc = jnp.zeros((16, d), F32).at[:bsz].set(c).at[bsz].set(c_ctx)
    mods = _mod_vectors(cc, w_mod, b_mod)
    rope = _rope_tables(seq)
    xc = jnp.concatenate([ctx, x], axis=1).reshape(bsz * nt_rows, d)
    params = dict(norm1_g=norm1_g, w_in=w_in, gla_w2=gla_w2, gla_b2=gla_b2, gla_norm_g=gla_norm_g,
                  q_norm_g=q_norm_g, k_norm_g=k_norm_g, att_sink=att_sink, ssd_conv_w=ssd_conv_w,
                  ssd_conv_b=ssd_conv_b, ssd_dt_bias=ssd_dt_bias, ssd_a_log=ssd_a_log, ssd_d=ssd_d,
                  ssd_norm_g=ssd_norm_g, gate_b=gate_b, w_br_a=w_br_a, w_br_b=w_br_b, w_br_c=w_br_c,
                  w_out=w_out, norm2_g=norm2_g, w1=w1, w3=w3, w2=w2)

    def mod_rows(l):
        m = mods[l]
        return jnp.stack([jnp.broadcast_to(m[bsz], (bsz, N_MOD * d)), m[:bsz]], axis=1).reshape(2 * bsz, 1, N_MOD * d)

    h = None
    for l in range(DEPTH):
        lp = {name: val[l] for name, val in params.items()}
        nxt = (norm1_g[l + 1], mod_rows(l + 1)) if l + 1 < DEPTH else None
        out = _layer(xc, h, mod_rows(l), lp, router_w, router_b, rope, bsz, nt_rows, n_ctx, nxt)
        xc, h = (out[0], out[1]) if nxt is not None else (out[0], None)
    return xc.reshape(bsz, seq, d)
```
